```python
import math
import jax
import jax.numpy as jnp
from jax import lax
import numpy as np

D_MODEL = 1024
BATCH = 4
SEQ = 4096
DEPTH = 4

GRID_W = 64
CTX_LEN = 256
N_MIXERS = 4
N_MOD = 9
D_FF = 2816
EPS = 1e-6
ROPE_BASE = 10000.0
ROPE_DIM = 64
Q_BLOCK = 128
NEG_INF = -1e30

MLA_HEADS = 8
MLA_Q_RANK = 512
MLA_KV_RANK = 256
MLA_NOPE = 128
MLA_ROPE = ROPE_DIM
MLA_V = 128

SWA_Q_HEADS = 16
SWA_KV_HEADS = 4
SWA_GROUP = SWA_Q_HEADS // SWA_KV_HEADS
SWA_HEAD_DIM = ROPE_DIM
SWA_WINDOW = 128

NA_HEADS = 16
NA_HEAD_DIM = D_MODEL // NA_HEADS
NA_KH = 8
NA_KW = 16

DIFF_HEADS = 8
DIFF_HEAD_DIM = ROPE_DIM

kernel_name = 'hybrid_mla_swa_na_diff_macaron_dit'


def rms_norm(x, g):
    xf = x.astype(jnp.float32)
    y = xf * lax.rsqrt(jnp.mean(xf * xf, axis=-1, keepdims=True) + EPS)
    return (y * g.astype(jnp.float32)).astype(x.dtype)


def modulate(h, shift, scale):
    return h * (1.0 + scale) + shift


def swiglu(h, w_in, w_out):
    g, u = jnp.split(h @ w_in, 2, axis=-1)
    return (jax.nn.silu(g) * u) @ w_out


def axial_rope_tables(L, dim):
    t = jnp.arange(L, dtype=jnp.int32)
    row = (t // GRID_W).astype(jnp.float32)
    col = (t % GRID_W).astype(jnp.float32)
    n_freq = dim // 4
    inv = jnp.exp(-math.log(ROPE_BASE) * jnp.arange(n_freq, dtype=jnp.float32) / n_freq)
    ang = jnp.concatenate([row[:, None] * inv, col[:, None] * inv], axis=-1)
    return jnp.cos(ang), jnp.sin(ang)


def apply_rope(x, cos, sin):
    half = x.shape[-1] // 2
    x1 = x[..., :half].astype(jnp.float32)
    x2 = x[..., half:].astype(jnp.float32)
    return jnp.concatenate([x1 * cos - x2 * sin, x1 * sin + x2 * cos], axis=-1).astype(x.dtype)


def softmax32(s):
    return jax.nn.softmax(s.astype(jnp.float32), axis=-1)


def joint_softmax(*scores):
    sizes = [s.shape[-1] for s in scores]
    p = softmax32(jnp.concatenate([s.astype(jnp.float32) for s in scores], axis=-1))
    cuts = [int(v) for v in np.cumsum(sizes)[:-1]]
    return jnp.split(p, cuts, axis=-1)


def to_blocks(t):
    lead = t.shape[:-2]
    L, d = t.shape[-2], t.shape[-1]
    t = t.reshape(lead + (L // Q_BLOCK, Q_BLOCK, d))
    return jnp.moveaxis(t, -3, 0)


def from_blocks(t):
    t = jnp.moveaxis(t, 0, -3)
    lead = t.shape[:-3]
    nb, qb, d = t.shape[-3], t.shape[-2], t.shape[-1]
    return t.reshape(lead + (nb * qb, d))


def merge_heads(o):
    B, H, N, d = o.shape
    return o.transpose(0, 2, 1, 3).reshape(B, N, H * d)


def pv(eq, p, v):
    return jnp.einsum(eq, p.astype(v.dtype), v)


def mla_mixer(hx, hc, w_down, q_norm, kv_norm, w_uq, w_ukv, w_o, cos, sin, ctx_out):
    H = MLA_HEADS
    scale = (MLA_NOPE + MLA_ROPE) ** -0.5

    def project(h):
        B, N, _ = h.shape
        d = h @ w_down
        cq = rms_norm(d[..., :MLA_Q_RANK], q_norm)
        ckv = rms_norm(d[..., MLA_Q_RANK:MLA_Q_RANK + MLA_KV_RANK], kv_norm)
        k_rope = d[..., MLA_Q_RANK + MLA_KV_RANK:]
        q = (cq @ w_uq).reshape(B, N, H, MLA_NOPE + MLA_ROPE).transpose(0, 2, 1, 3)
        kv = (ckv @ w_ukv).reshape(B, N, H, MLA_NOPE + MLA_V).transpose(0, 2, 1, 3)
        return q[..., :MLA_NOPE], q[..., MLA_NOPE:], kv[..., :MLA_NOPE], k_rope, kv[..., MLA_NOPE:]

    qn_x, qr_x, kn_x, kr_x, v_x = project(hx)
    qn_c, qr_c, kn_c, kr_c, v_c = project(hc)
    qr_x_rot = apply_rope(qr_x, cos, sin)
    kr_x_rot = apply_rope(kr_x, cos, sin)

    def scores(qn, qr, kn, kr):
        s = jnp.einsum('bhqd,bhkd->bhqk', qn, kn) + jnp.einsum('bhqr,bkr->bhqk', qr, kr)
        return s.astype(jnp.float32) * scale

    def block(args):
        qn, qr_rot, qr_pl = args
        p_x, p_c = joint_softmax(scores(qn, qr_rot, kn_x, kr_x_rot), scores(qn, qr_pl, kn_c, kr_c))
        return pv('bhqk,bhkd->bhqd', p_x, v_x) + pv('bhqk,bhkd->bhqd', p_c, v_c)

    o_x = from_blocks(lax.map(block, (to_blocks(qn_x), to_blocks(qr_x_rot), to_blocks(qr_x))))
    out_x = merge_heads(o_x) @ w_o
    out_c = None
    if ctx_out:
        p = softmax32(scores(qn_c, qr_c, kn_c, kr_c))
        out_c = merge_heads(pv('bhqk,bhkd->bhqd', p, v_c)) @ w_o
    return out_x, out_c


def swa_mixer(hx, hc, w_qkv, sink, w_o, cos, sin, ctx_out):
    Hk, G, Dh = SWA_KV_HEADS, SWA_GROUP, SWA_HEAD_DIM
    scale = Dh ** -0.5

    def project(h):
        B, N, _ = h.shape
        p = h @ w_qkv
        q = p[..., :Hk * G * Dh].reshape(B, N, Hk, G, Dh).transpose(0, 2, 3, 1, 4)
        k = p[..., Hk * G * Dh:(Hk * G + Hk) * Dh].reshape(B, N, Hk, Dh).transpose(0, 2, 1, 3)
        v = p[..., (Hk * G + Hk) * Dh:].reshape(B, N, Hk, Dh).transpose(0, 2, 1, 3)
        return q, k, v

    B, L, _ = hx.shape
    C = hc.shape[1]
    q_x, k_x, v_x = project(hx)
    q_c, k_c, v_c = project(hc)
    q_x_rot = apply_rope(q_x, cos, sin)
    k_x_rot = apply_rope(k_x, cos, sin)
    pad = ((0, 0), (0, 0), (SWA_WINDOW, SWA_WINDOW), (0, 0))
    k_pad = jnp.pad(k_x_rot, pad)
    v_pad = jnp.pad(v_x, pad)
    span = Q_BLOCK + 2 * SWA_WINDOW
    qi = jnp.arange(Q_BLOCK)
    kj = jnp.arange(span)
    sink_b = sink.reshape(Hk, G)[None, :, :, None, None].astype(jnp.float32)

    def ctx_scores(q):
        return jnp.einsum('bkgqd,bkcd->bkgqc', q, k_c).astype(jnp.float32) * scale

    def block(args):
        n, q_rot, q_pl = args
        start = n * Q_BLOCK
        k_b = lax.dynamic_slice_in_dim(k_pad, start, span, axis=2)
        v_b = lax.dynamic_slice_in_dim(v_pad, start, span, axis=2)
        qpos = start + qi
        kpos = start - SWA_WINDOW + kj
        valid = ((jnp.abs(kpos[None, :] - qpos[:, None]) <= SWA_WINDOW)
                 & (kpos[None, :] >= 0) & (kpos[None, :] < L))
        s_w = jnp.einsum('bkgqd,bkjd->bkgqj', q_rot, k_b).astype(jnp.float32) * scale
        s_w = jnp.where(valid, s_w, NEG_INF)
        s_c = ctx_scores(q_pl)
        s_s = jnp.broadcast_to(sink_b, s_c.shape[:-1] + (1,))
        p_w, p_c, _ = joint_softmax(s_w, s_c, s_s)
        return pv('bkgqj,bkjd->bkgqd', p_w, v_b) + pv('bkgqc,bkcd->bkgqd', p_c, v_c)

    nb = L // Q_BLOCK
    o = from_blocks(lax.map(block, (jnp.arange(nb), to_blocks(q_x_rot), to_blocks(q_x))))
    out_x = o.transpose(0, 3, 1, 2, 4).reshape(B, L, Hk * G * Dh) @ w_o
    out_c = None
    if ctx_out:
        s_c = ctx_scores(q_c)
        s_s = jnp.broadcast_to(sink_b, s_c.shape[:-1] + (1,))
        p_c, _ = joint_softmax(s_c, s_s)
        o_c = pv('bkgqc,bkcd->bkgqd', p_c, v_c)
        out_c = o_c.transpose(0, 3, 1, 2, 4).reshape(B, C, Hk * G * Dh) @ w_o
    return out_x, out_c


def na_mixer(hx, hc, w_qkv, rpb, w_o, ctx_out):
    H, Dh = NA_HEADS, NA_HEAD_DIM
    scale = Dh ** -0.5

    def project(h):
        B, N, _ = h.shape
        p = (h @ w_qkv).reshape(B, N, 3, H, Dh).transpose(2, 0, 3, 1, 4)
        return p[0], p[1], p[2]

    B, L, _ = hx.shape
    rows = L // GRID_W
    kh = min(NA_KH, rows)
    kw = NA_KW
    q_x, k_x, v_x = project(hx)
    q_c, k_c, v_c = project(hc)
    col = jnp.arange(GRID_W)
    col_start = jnp.clip(col - kw // 2, 0, GRID_W - kw)
    key_cols = col_start[:, None] + jnp.arange(kw)[None, :]
    col_off = key_cols - col[:, None] + (NA_KW - 1)

    def block(args):
        r, q_row = args
        row_start = jnp.clip(r - kh // 2, 0, rows - kh)
        key_rows = row_start + jnp.arange(kh)
        idx = (key_rows[None, :, None] * GRID_W + key_cols[:, None, :]).reshape(-1)
        k_g = jnp.take(k_x, idx, axis=2).reshape(B, H, GRID_W, kh * kw, Dh)
        v_g = jnp.take(v_x, idx, axis=2).reshape(B, H, GRID_W, kh * kw, Dh)
        row_off = key_rows - r + (NA_KH - 1)
        bias = rpb[:, row_off[None, :, None], col_off[:, None, :]]
        bias = bias.reshape(H, GRID_W, kh * kw).astype(jnp.float32)
        s_n = jnp.einsum('bhqd,bhqkd->bhqk', q_row, k_g).astype(jnp.float32) * scale + bias[None]
        s_c = jnp.einsum('bhqd,bhcd->bhqc', q_row, k_c).astype(jnp.float32) * scale
        p_n, p_c = joint_softmax(s_n, s_c)
        return pv('bhqk,bhqkd->bhqd', p_n, v_g) + pv('bhqc,bhcd->bhqd', p_c, v_c)

    q_rows = q_x.reshape(B, H, rows, GRID_W, Dh).transpose(2, 0, 1, 3, 4)
    o = lax.map(block, (jnp.arange(rows), q_rows))
    out_x = o.transpose(1, 0, 3, 2, 4).reshape(B, L, H * Dh) @ w_o
    out_c = None
    if ctx_out:
        p = softmax32(jnp.einsum('bhqd,bhcd->bhqc', q_c, k_c).astype(jnp.float32) * scale)
        out_c = merge_heads(pv('bhqc,bhcd->bhqd', p, v_c)) @ w_o
    return out_x, out_c


def diff_mixer(hx, hc, w_qkv, lam_params, norm_g, w_o, cos, sin, lam_init, ctx_out):
    H, Dh = DIFF_HEADS, DIFF_HEAD_DIM
    scale = Dh ** -0.5

    def project(h):
        B, N, _ = h.shape
        p = h @ w_qkv
        q = p[..., :H * 2 * Dh].reshape(B, N, H, 2, Dh).transpose(0, 2, 3, 1, 4)
        k = p[..., H * 2 * Dh:H * 4 * Dh].reshape(B, N, H, 2, Dh).transpose(0, 2, 3, 1, 4)
        v = p[..., H * 4 * Dh:].reshape(B, N, H, 2 * Dh).transpose(0, 2, 1, 3)
        return q, k, v

    q_x, k_x, v_x = project(hx)
    q_c, k_c, v_c = project(hc)
    q_x_rot = apply_rope(q_x, cos, sin)
    k_x_rot = apply_rope(k_x, cos, sin)
    lp = lam_params.astype(jnp.float32)
    lam = jnp.exp(jnp.sum(lp[0] * lp[1])) - jnp.exp(jnp.sum(lp[2] * lp[3])) + lam_init

    def scores(q, k):
        return jnp.einsum('bhiqd,bhikd->bhiqk', q, k).astype(jnp.float32) * scale

    def block(args):
        q_rot, q_pl = args
        p_x, p_c = joint_softmax(scores(q_rot, k_x_rot), scores(q_pl, k_c))
        a_x = p_x[:, :, 0] - lam * p_x[:, :, 1]
        a_c = p_c[:, :, 0] - lam * p_c[:, :, 1]
        return pv('bhqk,bhkd->bhqd', a_x, v_x) + pv('bhqk,bhkd->bhqd', a_c, v_c)

    def finish(o):
        return merge_heads(rms_norm(o, norm_g) * (1.0 - lam_init)) @ w_o

    o_x = from_blocks(lax.map(block, (to_blocks(q_x_rot), to_blocks(q_x))))
    out_x = finish(o_x)
    out_c = None
    if ctx_out:
        p = softmax32(scores(q_c, k_c))
        out_c = finish(pv('bhqk,bhkd->bhqd', p[:, :, 0] - lam * p[:, :, 1], v_c))
    return out_x, out_c


def setup_inputs(seed: int = 0) -> dict:
    key = jax.random.key(seed)
    keys = jax.random.split(key, 32)
    counter = [0]

    def normal(shape, std):
        k = keys[counter[0]]
        counter[0] += 1
        return jax.random.normal(k, shape, jnp.float32) * std

    def gain(shape):
        return 1.0 + normal(shape, 0.02)

    nA, nB, nC, nD = [len(range(m, DEPTH, N_MIXERS)) for m in range(N_MIXERS)]
    D = D_MODEL
    return {
        'x': normal((BATCH, SEQ, D), 1.0),
        'c': normal((BATCH, D), 1.0),
        'ctx': normal((BATCH, CTX_LEN, D), 1.0),
        'c_ctx': normal((D,), 1.0),
        'mod_w': normal((DEPTH, D, N_MOD * D), 0.5 * D ** -0.5),
        'mod_b': normal((DEPTH, N_MOD * D), 0.02),
        'norm_g': gain((DEPTH, 3, D)),
        'final_norm_g': gain((D,)),
        'ffn_w_in': normal((DEPTH, 2, D, 2 * D_FF), D ** -0.5),
        'ffn_w_out': normal((DEPTH, 2, D_FF, D), D_FF ** -0.5),
        'mla_w_down': normal((nA, D, MLA_Q_RANK + MLA_KV_RANK + MLA_ROPE), D ** -0.5),
        'mla_q_norm': gain((nA, MLA_Q_RANK)),
        'mla_kv_norm': gain((nA, MLA_KV_RANK)),
        'mla_w_uq': normal((nA, MLA_Q_RANK, MLA_HEADS * (MLA_NOPE + MLA_ROPE)), MLA_Q_RANK ** -0.5),
        'mla_w_ukv': normal((nA, MLA_KV_RANK, MLA_HEADS * (MLA_NOPE + MLA_V)), MLA_KV_RANK ** -0.5),
        'mla_w_o': normal((nA, MLA_HEADS * MLA_V, D), (MLA_HEADS * MLA_V) ** -0.5),
        'swa_w_qkv': normal((nB, D, (SWA_Q_HEADS + 2 * SWA_KV_HEADS) * SWA_HEAD_DIM), D ** -0.5),
        'swa_sink': normal((nB, SWA_Q_HEADS), 0.5),
        'swa_w_o': normal((nB, SWA_Q_HEADS * SWA_HEAD_DIM, D), (SWA_Q_HEADS * SWA_HEAD_DIM) ** -0.5),
        'na_w_qkv': normal((nC, D, 3 * NA_HEADS * NA_HEAD_DIM), D ** -0.5),
        'na_rpb': normal((nC, NA_HEADS, 2 * NA_KH - 1, 2 * NA_KW - 1), 0.2),
        'na_w_o': normal((nC, NA_HEADS * NA_HEAD_DIM, D), (NA_HEADS * NA_HEAD_DIM) ** -0.5),
        'diff_w_qkv': normal((nD, D, 6 * DIFF_HEADS * DIFF_HEAD_DIM), D ** -0.5),
        'diff_lambda': normal((nD, 4, DIFF_HEAD_DIM), 0.1),
        'diff_norm_g': gain((nD, 2 * DIFF_HEAD_DIM)),
        'diff_w_o': normal((nD, 2 * DIFF_HEADS * DIFF_HEAD_DIM, D), (2 * DIFF_HEADS * DIFF_HEAD_DIM) ** -0.5),
    }


def reference(x, c, ctx, c_ctx, mod_w, mod_b, norm_g, final_norm_g, ffn_w_in, ffn_w_out,
              mla_w_down, mla_q_norm, mla_kv_norm, mla_w_uq, mla_w_ukv, mla_w_o,
              swa_w_qkv, swa_sink, swa_w_o, na_w_qkv, na_rpb, na_w_o,
              diff_w_qkv, diff_lambda, diff_norm_g, diff_w_o):
    L = x.shape[1]
    cos, sin = axial_rope_tables(L, ROPE_DIM)
    xc = ctx
    cond_x = jax.nn.silu(c)
    cond_c = jax.nn.silu(c_ctx)
    for layer in range(DEPTH):
        kind = layer % N_MIXERS
        j = layer // N_MIXERS
        last = layer == DEPTH - 1
        mx = jnp.split((cond_x @ mod_w[layer] + mod_b[layer])[:, None, :], N_MOD, axis=-1)
        mc = jnp.split(cond_c @ mod_w[layer] + mod_b[layer], N_MOD, axis=-1)

        x = x + 0.5 * mx[2] * swiglu(modulate(rms_norm(x, norm_g[layer, 0]), mx[0], mx[1]),
                                     ffn_w_in[layer, 0], ffn_w_out[layer, 0])
        xc = xc + 0.5 * mc[2] * swiglu(modulate(rms_norm(xc, norm_g[layer, 0]), mc[0], mc[1]),
                                       ffn_w_in[layer, 0], ffn_w_out[layer, 0])

        hx = modulate(rms_norm(x, norm_g[layer, 1]), mx[3], mx[4])
        hc = modulate(rms_norm(xc, norm_g[layer, 1]), mc[3], mc[4])
        if kind == 0:
            ox, oc = mla_mixer(hx, hc, mla_w_down[j], mla_q_norm[j], mla_kv_norm[j], mla_w_uq[j],
                               mla_w_ukv[j], mla_w_o[j], cos, sin, not last)
        elif kind == 1:
            ox, oc = swa_mixer(hx, hc, swa_w_qkv[j], swa_sink[j], swa_w_o[j], cos, sin, not last)
        elif kind == 2:
            ox, oc = na_mixer(hx, hc, na_w_qkv[j], na_rpb[j], na_w_o[j], not last)
        else:
            lam_init = 0.8 - 0.6 * math.exp(-0.3 * layer)
            ox, oc = diff_mixer(hx, hc, diff_w_qkv[j], diff_lambda[j], diff_norm_g[j], diff_w_o[j],
                                cos, sin, lam_init, not last)
        x = x + mx[5] * ox

        x = x + 0.5 * mx[8] * swiglu(modulate(rms_norm(x, norm_g[layer, 2]), mx[6], mx[7]),
                                     ffn_w_in[layer, 1], ffn_w_out[layer, 1])
        if not last:
            xc = xc + mc[5] * oc
            xc = xc + 0.5 * mc[8] * swiglu(modulate(rms_norm(xc, norm_g[layer, 2]), mc[6], mc[7]),
                                           ffn_w_in[layer, 1], ffn_w_out[layer, 1])
    return rms_norm(x, final_norm_g)
```

```python
import functools
import math

import jax
import jax.numpy as jnp
import numpy as np
from jax import lax
from jax.experimental import pallas as pl
from jax.experimental.pallas import tpu as pltpu

F32 = jnp.float32
BF16 = jnp.bfloat16

GRID_W = 64
N_MOD = 9
EPS = 1e-6
ROPE_BASE = 10000.0
ROPE_DIM = 64
NEG_INF = -1e30

MLA_HEADS = 8
MLA_Q_RANK = 512
MLA_KV_RANK = 256
MLA_NOPE = 128
MLA_V = 128

SWA_KV_HEADS = 4
SWA_GROUP = 4
SWA_WINDOW = 128

NA_HEADS = 16
NA_KH = 8
NA_KW = 16

DIFF_HEADS = 8
HEAD_DIM = 64

LANES = 128
MXU_WIDTH = 256
VMEM_LIMIT_BYTES = 56 * 1024 * 1024
MOD_ROWS = 8


def _cparams(n_axes):
    return pltpu.CompilerParams(
        dimension_semantics=("arbitrary",) * n_axes,
        vmem_limit_bytes=VMEM_LIMIT_BYTES,
    )


def _resident(shape, index_map):
    return pl.BlockSpec(shape, index_map, pipeline_mode=pl.Buffered(1))


def _rms(x):
    return x * lax.rsqrt(jnp.mean(x * x, axis=-1, keepdims=True) + EPS)


def _norm_mod(x, g, mod_ref, base):
    y = _rms(x) * g
    if mod_ref is not None:
        y = y * (1.0 + mod_ref[base + 1:base + 2, :]) + mod_ref[base:base + 1, :]
    return y


def _dot(a, b):
    return jnp.dot(a, b, preferred_element_type=F32)


def _dot_nt(a, b):
    return lax.dot_general(a, b, (((1,), (1,)), ((), ())), preferred_element_type=F32)


def _lane_is_low(shape):
    return lax.broadcasted_iota(jnp.int32, shape, len(shape) - 1) < HEAD_DIM


def _stack_halves(q):
    low = _lane_is_low(q.shape)
    zero = jnp.zeros_like(q)
    return jnp.concatenate([jnp.where(low, q, zero), jnp.where(low, zero, q)], axis=0)


def _merge_halves(o):
    n = o.shape[0] // 2
    return jnp.where(_lane_is_low((n, o.shape[1])), o[:n], o[n:])


def _rope_tile(y, c, s):
    lane = lax.broadcasted_iota(jnp.int32, y.shape, 1)
    first = (lane & (HEAD_DIM - 1)) < HEAD_DIM // 2
    swapped = jnp.where(first, pltpu.roll(y, LANES - HEAD_DIM // 2, 1), pltpu.roll(y, HEAD_DIM // 2, 1))
    return y * c + swapped * s


class _Softmax:
    def __init__(self, m, l, acc):
        self.m, self.l, self.acc = m, l, acc

    @staticmethod
    def first(s, v):
        m = jnp.max(s, axis=-1, keepdims=True)
        p = jnp.exp(s - m)
        return _Softmax(m, jnp.sum(p, axis=-1, keepdims=True), _dot(p.astype(v.dtype), v))

    def update(self, s, v):
        m_new = jnp.maximum(self.m, jnp.max(s, axis=-1, keepdims=True))
        a = jnp.exp(self.m - m_new)
        p = jnp.exp(s - m_new)
        return _Softmax(m_new, a * self.l + jnp.sum(p, axis=-1, keepdims=True),
                        a * self.acc + _dot(p.astype(v.dtype), v))

    def carry(self):
        return self.m, self.l, self.acc


def _mod_body(c_ref, w_ref, b_ref, o_ref):
    c = c_ref[...]
    h = (c * jax.nn.sigmoid(c)).astype(BF16)
    o_ref[...] = _dot(h, w_ref[...].astype(BF16)) + b_ref[...]


def _mod_call(cond, mod_w, mod_b):
    depth, d, n = mod_w.shape
    tn = 1024
    return pl.pallas_call(
        _mod_body,
        grid=(depth, n // tn),
        in_specs=[
            pl.BlockSpec((MOD_ROWS, d), lambda l, j: (0, 0)),
            pl.BlockSpec((None, d, tn), lambda l, j: (l, 0, j)),
            pl.BlockSpec((None, 1, tn), lambda l, j: (l, 0, j)),
        ],
        out_specs=pl.BlockSpec((None, MOD_ROWS, tn), lambda l, j: (l, 0, j)),
        out_shape=jax.ShapeDtypeStruct((depth, MOD_ROWS, n), F32),
        compiler_params=_cparams(2),
        name="adaln_mod",
    )(cond, mod_w, mod_b.reshape(depth, 1, n))


def _ffn_body(x_ref, g_ref, mod_ref, win_ref, wout_ref, *rest, base, ff, chunk, final):
    if final:
        fg_ref, o_ref, act_ref = rest
    else:
        o_ref, act_ref = rest
    x = x_ref[...]
    hb = _norm_mod(x, g_ref[...], mod_ref, base).astype(BF16)
    for c0 in range(0, ff, chunk):
        g = _dot(hb, win_ref[:, c0:c0 + chunk])
        u = _dot(hb, win_ref[:, ff + c0:ff + c0 + chunk])
        act_ref[:, c0:c0 + chunk] = (g * jax.nn.sigmoid(g) * u).astype(BF16)
    y = x + (0.5 * mod_ref[base + 2:base + 3, :]) * _dot(act_ref[...], wout_ref[...])
    if final:
        y = _rms(y) * fg_ref[...]
    o_ref[...] = y


def _ffn(x, g, mods_l, mod_row, w_in, w_out, base, final_g=None):
    bx, n, d = x.shape
    ff = w_out.shape[0]
    tm = min(512, n)
    final = final_g is not None
    in_specs = [
        pl.BlockSpec((None, tm, d), lambda b, i: (b, i, 0)),
        pl.BlockSpec((1, d), lambda b, i: (0, 0)),
        pl.BlockSpec((None, N_MOD, d), lambda b, i: (mod_row(b), 0, 0)),
        _resident((d, 2 * ff), lambda b, i: (0, 0)),
        _resident((ff, d), lambda b, i: (0, 0)),
    ]
    args = [x, g.reshape(1, d), mods_l, w_in, w_out]
    if final:
        in_specs.append(pl.BlockSpec((1, d), lambda b, i: (0, 0)))
        args.append(final_g.reshape(1, d))
    return pl.pallas_call(
        functools.partial(_ffn_body, base=base, ff=ff, chunk=MXU_WIDTH, final=final),
        grid=(bx, n // tm),
        in_specs=in_specs,
        out_specs=pl.BlockSpec((None, tm, d), lambda b, i: (b, i, 0)),
        out_shape=jax.ShapeDtypeStruct(x.shape, F32),
        scratch_shapes=[pltpu.VMEM((tm, ff), BF16)],
        compiler_params=_cparams(2),
        name="ffn_swiglu",
    )(*args)


def _proj_body(*refs, base, use_mod, kinds):
    x_ref, g_ref = refs[0], refs[1]
    pos = 2
    mod_ref = None
    if use_mod:
        mod_ref = refs[pos]
        pos += 1
    w_ref = refs[pos]
    pos += 1
    cs_ref = None
    if any(k is not None for k in kinds):
        cs_ref = refs[pos]
        pos += 1
    o_ref = refs[pos]
    hb = _norm_mod(x_ref[...].astype(F32), g_ref[...], mod_ref, base).astype(BF16)
    n_tiles = len(kinds)
    t = 0
    while t < n_tiles:
        w = min(MXU_WIDTH // LANES, n_tiles - t)
        y = _dot(hb, w_ref[:, t * LANES:(t + w) * LANES])
        for j in range(w):
            yj = y[:, j * LANES:(j + 1) * LANES]
            kind = kinds[t + j]
            if kind is not None:
                yj = _rope_tile(yj, cs_ref[2 * kind], cs_ref[2 * kind + 1])
            o_ref[:, (t + j) * LANES:(t + j + 1) * LANES] = yj.astype(o_ref.dtype)
        t += w


def _proj(x, in_width, in_block, g, mods_l, mod_row, base, w, kinds, cs, out_dtype):
    bx, n, _ = x.shape
    f = w.shape[1]
    tm = min(512, n)
    use_mod = mods_l is not None
    use_rope = any(k is not None for k in kinds)
    in_specs = [
        pl.BlockSpec((None, tm, in_width), lambda b, i: (b, i, in_block)),
        pl.BlockSpec((1, in_width), lambda b, i: (0, 0)),
    ]
    args = [x, g.reshape(1, in_width)]
    if use_mod:
        in_specs.append(pl.BlockSpec((None, N_MOD, in_width), lambda b, i: (mod_row(b), 0, 0)))
        args.append(mods_l)
    in_specs.append(_resident((in_width, f), lambda b, i: (0, 0)))
    args.append(w)
    if use_rope:
        in_specs.append(pl.BlockSpec((4, tm, LANES), lambda b, i: (0, i, 0)))
        args.append(cs)
    return pl.pallas_call(
        functools.partial(_proj_body, base=base, use_mod=use_mod, kinds=tuple(kinds)),
        grid=(bx, n // tm),
        in_specs=in_specs,
        out_specs=pl.BlockSpec((None, tm, f), lambda b, i: (b, i, 0)),
        out_shape=jax.ShapeDtypeStruct((bx, n, f), out_dtype),
        compiler_params=_cparams(2),
        name="norm_proj",
    )(*args)


def _oproj_body(x_ref, a_ref, w_ref, mod_ref, o_ref, *, base):
    o_ref[...] = x_ref[...] + mod_ref[base:base + 1, :] * _dot(a_ref[...], w_ref[...])


def _oproj(x, a, w, mods_l, mod_row, base):
    bx, n, d = x.shape
    k = a.shape[-1]
    tm = min(512, n)
    return pl.pallas_call(
        functools.partial(_oproj_body, base=base),
        grid=(bx, n // tm),
        in_specs=[
            pl.BlockSpec((None, tm, d), lambda b, i: (b, i, 0)),
            pl.BlockSpec((None, tm, k), lambda b, i: (b, i, 0)),
            _resident((k, d), lambda b, i: (0, 0)),
            pl.BlockSpec((None, N_MOD, d), lambda b, i: (mod_row(b), 0, 0)),
        ],
        out_specs=pl.BlockSpec((None, tm, d), lambda b, i: (b, i, 0)),
        out_shape=jax.ShapeDtypeStruct(x.shape, F32),
        compiler_params=_cparams(2),
        name="out_proj_residual",
    )(x, a, w, mods_l)


def _mla_body(*refs, scale, tk, has_lat):
    if has_lat:
        qn_ref, qr_ref, kn_ref, kr_ref, v_ref, knc_ref, krc_ref, vc_ref, o_ref, kcat_ref = refs
    else:
        qn_ref, qr_ref, knc_ref, krc_ref, vc_ref, o_ref = refs
    q = jnp.concatenate([qn_ref[...], qr_ref[...]], axis=1)
    kc = jnp.concatenate([knc_ref[...], krc_ref[...].astype(BF16)], axis=1)
    st = _Softmax.first(_dot_nt(q, kc) * scale, vc_ref[...])
    if has_lat:
        @pl.when(pl.program_id(2) == 0)
        def _():
            kcat_ref[:, :LANES] = kn_ref[...]
            kcat_ref[:, LANES:] = kr_ref[...].astype(BF16)

        def step(j, carry):
            r = pl.ds(pl.multiple_of(j * tk, tk), tk)
            return _Softmax(*carry).update(_dot_nt(q, kcat_ref[r, :]) * scale, v_ref[r, :]).carry()

        st = _Softmax(*lax.fori_loop(0, kn_ref.shape[0] // tk, step, st.carry()))
    o_ref[...] = (st.acc / st.l).astype(o_ref.dtype)


def _mla_attn(q2, kv2c, p1c, kv2=None, p1=None):
    b, lq, _ = q2.shape
    h = MLA_HEADS
    c = kv2c.shape[1]
    has_lat = kv2 is not None
    tq = min(256, lq)
    rope_tile = (MLA_Q_RANK + MLA_KV_RANK) // LANES
    scale = (MLA_NOPE + ROPE_DIM) ** -0.5
    in_specs = [
        pl.BlockSpec((None, tq, LANES), lambda bi, hi, i: (bi, i, hi)),
        pl.BlockSpec((None, tq, LANES), lambda bi, hi, i: (bi, i, h + hi)),
    ]
    args = [q2, q2]
    scratch = []
    tk = 0
    if has_lat:
        lk = kv2.shape[1]
        tk = min(512, lk)
        in_specs += [
            pl.BlockSpec((None, lk, LANES), lambda bi, hi, i: (bi, 0, hi)),
            pl.BlockSpec((None, lk, LANES), lambda bi, hi, i: (bi, 0, rope_tile)),
            pl.BlockSpec((None, lk, LANES), lambda bi, hi, i: (bi, 0, h + hi)),
        ]
        args += [kv2, p1, kv2]
        scratch = [pltpu.VMEM((lk, 2 * LANES), BF16)]
    in_specs += [
        pl.BlockSpec((None, c, LANES), lambda bi, hi, i: (bi, 0, hi)),
        pl.BlockSpec((None, c, LANES), lambda bi, hi, i: (bi, 0, rope_tile)),
        pl.BlockSpec((None, c, LANES), lambda bi, hi, i: (bi, 0, h + hi)),
    ]
    args += [kv2c, p1c, kv2c]
    return pl.pallas_call(
        functools.partial(_mla_body, scale=scale, tk=tk, has_lat=has_lat),
        grid=(b, h, lq // tq),
        in_specs=in_specs,
        out_specs=pl.BlockSpec((None, tq, LANES), lambda bi, hi, i: (bi, i, hi)),
        out_shape=jax.ShapeDtypeStruct((b, lq, h * MLA_V), BF16),
        scratch_shapes=scratch,
        compiler_params=_cparams(3),
        name="mla_attention",
    )(*args)


def _diff_body(*refs, lam_init, tk, has_lat):
    if has_lat:
        qr_ref, qp_ref, k_ref, v_ref, kc_ref, vc_ref, lam_ref, ng_ref, o_ref = refs
    else:
        qp_ref, kc_ref, vc_ref, lam_ref, ng_ref, o_ref = refs
    scale = HEAD_DIM ** -0.5
    tq = qp_ref.shape[0]
    st = _Softmax.first(_dot_nt(_stack_halves(qp_ref[...]), kc_ref[...]) * scale, vc_ref[...])
    if has_lat:
        q = _stack_halves(qr_ref[...])

        def step(j, carry):
            r = pl.ds(pl.multiple_of(j * tk, tk), tk)
            return _Softmax(*carry).update(_dot_nt(q, k_ref[r, :]) * scale, v_ref[r, :]).carry()

        st = _Softmax(*lax.fori_loop(0, k_ref.shape[0] // tk, step, st.carry()))
    lp = lam_ref[...]
    lam = (jnp.exp(jnp.sum(lp[0:1] * lp[1:2], axis=-1, keepdims=True))
           - jnp.exp(jnp.sum(lp[2:3] * lp[3:4], axis=-1, keepdims=True)) + lam_init)
    o = st.acc / st.l
    o = o[:tq] - lam * o[tq:]
    o_ref[...] = (_rms(o) * ng_ref[...] * (1.0 - lam_init)).astype(o_ref.dtype)


def _diff_attn(pc, q_col_c, k_col_c, v_col_c, lam_params, norm_g, lam_init, px=None):
    h = DIFF_HEADS
    has_lat = px is not None
    src = px if has_lat else pc
    b, lq, _ = src.shape
    c = pc.shape[1]
    tq = min(256, lq)
    in_specs, args, tk = [], [], 0
    if has_lat:
        lk = px.shape[1]
        tk = min(512, lk)
        in_specs += [
            pl.BlockSpec((None, tq, LANES), lambda bi, hi, i: (bi, i, hi)),
            pl.BlockSpec((None, tq, LANES), lambda bi, hi, i: (bi, i, 2 * h + hi)),
            pl.BlockSpec((None, lk, LANES), lambda bi, hi, i: (bi, 0, h + hi)),
            pl.BlockSpec((None, lk, LANES), lambda bi, hi, i: (bi, 0, 3 * h + hi)),
        ]
        args += [px, px, px, px]
    else:
        in_specs.append(pl.BlockSpec((None, tq, LANES), lambda bi, hi, i: (bi, i, q_col_c + hi)))
        args.append(pc)
    in_specs += [
        pl.BlockSpec((None, c, LANES), lambda bi, hi, i: (bi, 0, k_col_c + hi)),
        pl.BlockSpec((None, c, LANES), lambda bi, hi, i: (bi, 0, v_col_c + hi)),
        pl.BlockSpec((4, HEAD_DIM), lambda bi, hi, i: (0, 0)),
        pl.BlockSpec((1, 2 * HEAD_DIM), lambda bi, hi, i: (0, 0)),
    ]
    args += [pc, pc, lam_params, norm_g.reshape(1, 2 * HEAD_DIM)]
    return pl.pallas_call(
        functools.partial(_diff_body, lam_init=lam_init, tk=tk, has_lat=has_lat),
        grid=(b, h, lq // tq),
        in_specs=in_specs,
        out_specs=pl.BlockSpec((None, tq, LANES), lambda bi, hi, i: (bi, i, hi)),
        out_shape=jax.ShapeDtypeStruct((b, lq, h * 2 * HEAD_DIM), BF16),
        compiler_params=_cparams(3),
        name="diff_attention",
    )(*args)


def _swa_body(*refs, tq, has_lat):
    if has_lat:
        sink_ref, qr_ref, qp_ref, k_ref, v_ref, kc_ref, vc_ref, o_ref = refs
    else:
        sink_ref, qp_ref, kc_ref, vc_ref, o_ref = refs
    scale = HEAD_DIM ** -0.5
    pair = pl.program_id(1)
    kc, vc = kc_ref[...], vc_ref[...]
    if has_lat:
        lk = k_ref.shape[0]
        span = tq + 2 * SWA_WINDOW
        q0 = pl.program_id(2) * tq
        start = pl.multiple_of(jnp.clip(q0 - SWA_WINDOW, 0, lk - span), SWA_WINDOW)
        kw = k_ref[pl.ds(start, span), :]
        vw = v_ref[pl.ds(start, span), :]
        qpos = q0 + lax.broadcasted_iota(jnp.int32, (tq, span), 0)
        kpos = start + lax.broadcasted_iota(jnp.int32, (tq, span), 1)
        valid = jnp.abs(kpos - qpos) <= SWA_WINDOW
    low = _lane_is_low((tq, LANES))
    for g in range(SWA_GROUP):
        cols = slice(g * LANES, (g + 1) * LANES)
        qp = qp_ref[:, cols]
        zero = jnp.zeros_like(qp)
        outs = []
        for e in range(2):
            sink = sink_ref[(2 * pair + e) * SWA_GROUP + g]
            keep = low if e == 0 else jnp.logical_not(low)
            s_c = _dot_nt(jnp.where(keep, qp, zero), kc) * scale
            m = jnp.maximum(jnp.max(s_c, axis=-1, keepdims=True), sink)
            if has_lat:
                s_w = _dot_nt(jnp.where(keep, qr_ref[:, cols], zero), kw) * scale
                s_w = jnp.where(valid, s_w, NEG_INF)
                m = jnp.maximum(m, jnp.max(s_w, axis=-1, keepdims=True))
            p_c = jnp.exp(s_c - m)
            l = jnp.sum(p_c, axis=-1, keepdims=True) + jnp.exp(sink - m)
            o = _dot(p_c.astype(BF16), vc)
            if has_lat:
                p_w = jnp.exp(s_w - m)
                l = l + jnp.sum(p_w, axis=-1, keepdims=True)
                o = o + _dot(p_w.astype(BF16), vw)
            outs.append(o / l)
        o_ref[:, cols] = jnp.where(low, outs[0], outs[1]).astype(o_ref.dtype)


def _swa_attn(sink, pc, q_col_c, k_col_c, v_col_c, px=None):
    n_pairs = SWA_KV_HEADS // 2
    qw = SWA_GROUP * LANES
    has_lat = px is not None
    src = px if has_lat else pc
    b, lq, _ = src.shape
    c = pc.shape[1]
    tq = min(256, lq)
    in_specs = [pl.BlockSpec(memory_space=pltpu.SMEM)]
    args = [sink]
    if has_lat:
        lk = px.shape[1]
        q_tiles = n_pairs * SWA_GROUP
        in_specs += [
            pl.BlockSpec((None, tq, qw), lambda bi, p, i: (bi, i, p)),
            pl.BlockSpec((None, tq, qw), lambda bi, p, i: (bi, i, n_pairs + p)),
            pl.BlockSpec((None, lk, LANES), lambda bi, p, i: (bi, 0, 2 * q_tiles + p)),
            pl.BlockSpec((None, lk, LANES), lambda bi, p, i: (bi, 0, 2 * q_tiles + n_pairs + p)),
        ]
        args += [px, px, px, px]
    else:
        in_specs.append(pl.BlockSpec((None, tq, qw), lambda bi, p, i: (bi, i, q_col_c // SWA_GROUP + p)))
        args.append(pc)
    in_specs += [
        pl.BlockSpec((None, c, LANES), lambda bi, p, i: (bi, 0, k_col_c + p)),
        pl.BlockSpec((None, c, LANES), lambda bi, p, i: (bi, 0, v_col_c + p)),
    ]
    args += [pc, pc]
    return pl.pallas_call(
        functools.partial(_swa_body, tq=tq, has_lat=has_lat),
        grid=(b, n_pairs, lq // tq),
        in_specs=in_specs,
        out_specs=pl.BlockSpec((None, tq, qw), lambda bi, p, i: (bi, i, p)),
        out_shape=jax.ShapeDtypeStruct((b, lq, n_pairs * qw), BF16),
        compiler_params=_cparams(3),
        name="swa_attention",
    )(*args)


def _na_body(*refs, rows_per_tile, n_rows, has_lat):
    if has_lat:
        q_ref, k_ref, v_ref, bias_ref, kc_ref, vc_ref, o_ref = refs
    else:
        q_ref, kc_ref, vc_ref, o_ref = refs
    scale = HEAD_DIM ** -0.5
    kc, vc = kc_ref[...], vc_ref[...]
    if not has_lat:
        st = _Softmax.first(_dot_nt(_stack_halves(q_ref[...]), kc) * scale, vc)
        o_ref[...] = _merge_halves(st.acc / st.l).astype(o_ref.dtype)
        return
    n_keys = NA_KH * GRID_W
    tile = pl.program_id(2)

    def row(rr, carry):
        r = tile * rows_per_tile + rr
        row_start = jnp.clip(r - NA_KH // 2, 0, n_rows - NA_KH)
        off = row_start - r + (NA_KH - 1)
        keys = pl.ds(pl.multiple_of(row_start * GRID_W, GRID_W), n_keys)
        qrows = pl.ds(pl.multiple_of(rr * GRID_W, GRID_W), GRID_W)
        q = _stack_halves(q_ref[qrows, :])
        bias = jnp.concatenate([bias_ref[0, off], bias_ref[1, off]], axis=0)
        st = _Softmax.first(_dot_nt(q, kc) * scale, vc)
        st = st.update(_dot_nt(q, k_ref[keys, :]) * scale + bias, v_ref[keys, :])
        o_ref[qrows, :] = _merge_halves(st.acc / st.l).astype(o_ref.dtype)
        return carry

    lax.fori_loop(0, rows_per_tile, row, 0)


def _na_attn(pc, bias=None, px=None):
    n_pairs = NA_HEADS // 2
    has_lat = px is not None
    src = px if has_lat else pc
    b, lq, _ = src.shape
    c = pc.shape[1]
    in_specs = []
    args = []
    if has_lat:
        n_rows = lq // GRID_W
        rows_per_tile = min(8, n_rows)
        tq = rows_per_tile * GRID_W
        in_specs += [
            pl.BlockSpec((None, tq, LANES), lambda bi, p, i: (bi, i, p)),
            pl.BlockSpec((None, lq, LANES), lambda bi, p, i: (bi, 0, n_pairs + p)),
            pl.BlockSpec((None, lq, LANES), lambda bi, p, i: (bi, 0, 2 * n_pairs + p)),
            pl.BlockSpec((2, NA_KH, GRID_W, NA_KH * GRID_W), lambda bi, p, i: (p, 0, 0, 0)),
        ]
        args += [px, px, px, bias]
    else:
        n_rows, rows_per_tile, tq = 0, 0, lq
        in_specs.append(pl.BlockSpec((None, tq, LANES), lambda bi, p, i: (bi, i, p)))
        args.append(pc)
    in_specs += [
        pl.BlockSpec((None, c, LANES), lambda bi, p, i: (bi, 0, n_pairs + p)),
        pl.BlockSpec((None, c, LANES), lambda bi, p, i: (bi, 0, 2 * n_pairs + p)),
    ]
    args += [pc, pc]
    return pl.pallas_call(
        functools.partial(_na_body, rows_per_tile=rows_per_tile, n_rows=n_rows, has_lat=has_lat),
        grid=(b, n_pairs, lq // tq),
        in_specs=in_specs,
        out_specs=pl.BlockSpec((None, tq, LANES), lambda bi, p, i: (bi, i, p)),
        out_shape=jax.ShapeDtypeStruct((b, lq, NA_HEADS * HEAD_DIM), BF16),
        compiler_params=_cparams(3),
        name="na_attention",
    )(*args)


def _rope_tables(seq):
    t = jnp.arange(seq, dtype=jnp.int32)
    row = (t // GRID_W).astype(F32)
    col = (t % GRID_W).astype(F32)
    n_freq = ROPE_DIM // 4
    inv = jnp.exp(-math.log(ROPE_BASE) * jnp.arange(n_freq, dtype=F32) / n_freq)
    ang = jnp.concatenate([row[:, None] * inv, col[:, None] * inv], axis=-1)
    cos, sin = jnp.cos(ang), jnp.sin(ang)
    c_head = jnp.concatenate([cos, cos], axis=-1)
    s_head = jnp.concatenate([-sin, sin], axis=-1)
    one, zero = jnp.ones_like(c_head), jnp.zeros_like(s_head)
    return jnp.stack([
        jnp.concatenate([c_head, c_head], axis=-1), jnp.concatenate([s_head, s_head], axis=-1),
        jnp.concatenate([c_head, one], axis=-1), jnp.concatenate([s_head, zero], axis=-1),
    ])


def _na_bias_table(rpb):
    col = np.arange(GRID_W)
    col_start = np.clip(col - NA_KW // 2, 0, GRID_W - NA_KW)
    in_win = (col[None, :] >= col_start[:, None]) & (col[None, :] < col_start[:, None] + NA_KW)
    cidx = np.clip(col[None, :] - col[:, None] + (NA_KW - 1), 0, 2 * NA_KW - 2)
    ridx = np.arange(NA_KH)[:, None] + np.arange(NA_KH)[None, :]
    t = rpb.astype(F32)[:, ridx][:, :, :, cidx]
    t = jnp.where(in_win[None, None, None], t, NEG_INF)
    h = rpb.shape[0]
    return t.transpose(0, 1, 3, 2, 4).reshape(h, NA_KH, GRID_W, NA_KH * GRID_W)


def _swa_q_perm():
    idx = []
    for pair in range(SWA_KV_HEADS // 2):
        for g in range(SWA_GROUP):
            for e in range(2):
                head = (2 * pair + e) * SWA_GROUP + g
                idx.extend(range(head * HEAD_DIM, (head + 1) * HEAD_DIM))
    return np.asarray(idx)


def kernel(x, c, ctx, c_ctx, mod_w, mod_b, norm_g, final_norm_g, ffn_w_in, ffn_w_out,
           mla_w_down, mla_q_norm, mla_kv_norm, mla_w_uq, mla_w_ukv, mla_w_o,
           swa_w_qkv, swa_sink, swa_w_o, na_w_qkv, na_rpb, na_w_o,
           diff_w_qkv, diff_lambda, diff_norm_g, diff_w_o):
    bsz, seq, d = x.shape
    n_ctx = ctx.shape[1]
    depth = mod_w.shape[0]
    assert bsz + 1 <= MOD_ROWS and seq % GRID_W == 0

    cond = jnp.zeros((MOD_ROWS, d), F32).at[:bsz].set(c).at[bsz].set(c_ctx)
    mods = _mod_call(cond, mod_w, mod_b).reshape(depth, MOD_ROWS, N_MOD, d)
    cs = _rope_tables(seq)
    xc = ctx.reshape(1, bsz * n_ctx, d)
    x_row = lambda b: b
    c_row = lambda b: bsz
    ROT, HALF = 0, 1

    def per_batch(t):
        return t.reshape(bsz, n_ctx, t.shape[-1])

    for layer in range(depth):
        kind = layer % 4
        j = layer // 4
        last = layer == depth - 1
        ml = mods[layer]
        w_in = ffn_w_in[layer].astype(BF16)
        w_out = ffn_w_out[layer].astype(BF16)

        x = _ffn(x, norm_g[layer, 0], ml, x_row, w_in[0], w_out[0], 0)
        xc = _ffn(xc, norm_g[layer, 0], ml, c_row, w_in[0], w_out[0], 0)
        g1 = norm_g[layer, 1]

        if kind == 0:
            h = MLA_HEADS
            qk = MLA_Q_RANK + MLA_KV_RANK
            wd = mla_w_down[j].astype(BF16)
            z = jnp.zeros((d, ROPE_DIM), BF16)
            w1x = jnp.concatenate([wd, z], axis=1)
            w1c = jnp.concatenate([wd[:, :qk], z, wd[:, qk:]], axis=1)
            n1 = w1x.shape[1] // LANES
            p1x = _proj(x, d, 0, g1, ml, x_row, 3, w1x, [None] * (n1 - 1) + [ROT], cs, F32)
            p1c = _proj(xc, d, 0, g1, ml, c_row, 3, w1c, [None] * n1, None, F32)
            wq = mla_w_uq[j].astype(BF16).reshape(MLA_Q_RANK, h, MLA_NOPE + ROPE_DIM)
            wq_rope = jnp.concatenate([wq[:, :, MLA_NOPE:]] * 2, axis=-1)
            wq2 = jnp.concatenate([wq[:, :, :MLA_NOPE].reshape(MLA_Q_RANK, -1),
                                   wq_rope.reshape(MLA_Q_RANK, -1)], axis=1)
            wkv = mla_w_ukv[j].astype(BF16).reshape(MLA_KV_RANK, h, MLA_NOPE + MLA_V)
            wkv2 = jnp.concatenate([wkv[:, :, :MLA_NOPE].reshape(MLA_KV_RANK, -1),
                                    wkv[:, :, MLA_NOPE:].reshape(MLA_KV_RANK, -1)], axis=1)
            none2 = [None] * (2 * h)
            q2x = _proj(p1x, MLA_Q_RANK, 0, mla_q_norm[j], None, None, 0, wq2, [None] * h + [HALF] * h, cs, BF16)
            kv2x = _proj(p1x, MLA_KV_RANK, MLA_Q_RANK // MLA_KV_RANK, mla_kv_norm[j], None, None, 0, wkv2, none2, None, BF16)
            kv2c = per_batch(_proj(p1c, MLA_KV_RANK, MLA_Q_RANK // MLA_KV_RANK, mla_kv_norm[j], None, None, 0, wkv2, none2, None, BF16))
            p1c_b = per_batch(p1c)
            ox = _mla_attn(q2x, kv2c, p1c_b, kv2x, p1x)
            w_o = mla_w_o[j].astype(BF16)
            if not last:
                q2c = per_batch(_proj(p1c, MLA_Q_RANK, 0, mla_q_norm[j], None, None, 0, wq2, none2, None, BF16))
                oc = _mla_attn(q2c, kv2c, p1c_b)
        elif kind == 1:
            wqkv = swa_w_qkv[j].astype(BF16)
            nq = SWA_KV_HEADS * SWA_GROUP * HEAD_DIM
            nk = SWA_KV_HEADS * HEAD_DIM
            perm = _swa_q_perm()
            wq = wqkv[:, :nq][:, perm]
            wk, wv = wqkv[:, nq:nq + nk], wqkv[:, nq + nk:]
            tq_, tk_ = nq // LANES, nk // LANES
            px = _proj(x, d, 0, g1, ml, x_row, 3, jnp.concatenate([wq, wq, wk, wv], axis=1),
                       [ROT] * tq_ + [None] * tq_ + [ROT] * tk_ + [None] * tk_, cs, BF16)
            pc = per_batch(_proj(xc, d, 0, g1, ml, c_row, 3, jnp.concatenate([wq, wk, wv], axis=1),
                                 [None] * (tq_ + 2 * tk_), None, BF16))
            ox = _swa_attn(swa_sink[j], pc, 0, tq_, tq_ + tk_, px)
            w_o = swa_w_o[j].astype(BF16)[perm, :]
            if not last:
                oc = _swa_attn(swa_sink[j], pc, 0, tq_, tq_ + tk_)
        elif kind == 2:
            wqkv = na_w_qkv[j].astype(BF16)
            nt = wqkv.shape[1] // LANES
            px = _proj(x, d, 0, g1, ml, x_row, 3, wqkv, [None] * nt, None, BF16)
            pc = per_batch(_proj(xc, d, 0, g1, ml, c_row, 3, wqkv, [None] * nt, None, BF16))
            ox = _na_attn(pc, _na_bias_table(na_rpb[j]), px)
            w_o = na_w_o[j].astype(BF16)
            if not last:
                oc = _na_attn(pc)
        else:
            h = DIFF_HEADS
            lam_init = 0.8 - 0.6 * math.exp(-0.3 * layer)
            wqkv = diff_w_qkv[j].astype(BF16)
            nq = 2 * h * HEAD_DIM
            wq, wk, wv = wqkv[:, :nq], wqkv[:, nq:2 * nq], wqkv[:, 2 * nq:]
            px = _proj(x, d, 0, g1, ml, x_row, 3, jnp.concatenate([wq, wk, wq, wv], axis=1),
                       [ROT] * (2 * h) + [None] * (2 * h), cs, BF16)
            pc = per_batch(_proj(xc, d, 0, g1, ml, c_row, 3, wqkv, [None] * (3 * h), None, BF16))
            ox = _diff_attn(pc, 0, h, 2 * h, diff_lambda[j], diff_norm_g[j], lam_init, px)
            w_o = diff_w_o[j].astype(BF16)
            if not last:
                oc = _diff_attn(pc, 0, h, 2 * h, diff_lambda[j], diff_norm_g[j], lam_init)

        x = _oproj(x, ox, w_o, ml, x_row, 5)
        x = _ffn(x, norm_g[layer, 2], ml, x_row, w_in[1], w_out[1], 6,
                 final_g=final_norm_g if last else None)
        if not last:
            xc = _oproj(xc, oc.reshape(1, bsz * n_ctx, oc.shape[-1]), w_o, ml, c_row, 5)
            xc = _ffn(xc, norm_g[layer, 2], ml, c_row, w_in[1], w_out[1], 6)
    return x
```

```python
import functools
import math

import jax
import jax.numpy as jnp
import numpy as np
from jax import lax
from jax.experimental import pallas as pl
from jax.experimental.pallas import tpu as pltpu

F32 = jnp.float32
BF16 = jnp.bfloat16

GRID_W = 64
N_MOD = 9
EPS = 1e-6
ROPE_BASE = 10000.0
ROPE_DIM = 64
NEG_INF = -1e30
LOG2E = math.log2(math.e)

MLA_HEADS = 8
MLA_Q_RANK = 512
MLA_KV_RANK = 256
MLA_NOPE = 128
MLA_V = 128

SWA_KV_HEADS = 4
SWA_GROUP = 4
SWA_WINDOW = 128

NA_HEADS = 16
NA_KH = 8
NA_KW = 16
NA_STEP_ROWS = 2
NA_WIN_ROWS = NA_KH + NA_STEP_ROWS
NA_CLASSES = 5

DIFF_HEADS = 8
HEAD_DIM = 64

LANES = 128
MXU_WIDTH = 256
VMEM_LIMIT_BYTES = 56 * 1024 * 1024
MOD_ROWS = 8


def _cparams(n_axes):
    return pltpu.CompilerParams(
        dimension_semantics=("arbitrary",) * n_axes,
        vmem_limit_bytes=VMEM_LIMIT_BYTES,
    )


def _resident(shape, index_map):
    return pl.BlockSpec(shape, index_map, pipeline_mode=pl.Buffered(1))


def _rms(x):
    return x * lax.rsqrt(jnp.mean(x * x, axis=-1, keepdims=True) + EPS)


def _norm_mod(x, g, mod_ref, base):
    y = _rms(x) * g
    if mod_ref is not None:
        y = y * (1.0 + mod_ref[base + 1:base + 2, :]) + mod_ref[base:base + 1, :]
    return y


def _dot(a, b):
    return jnp.dot(a, b, preferred_element_type=F32)


def _dot_nt(a, b):
    return lax.dot_general(a, b, (((1,), (1,)), ((), ())), preferred_element_type=F32)


def _lane_is_low(shape):
    return lax.broadcasted_iota(jnp.int32, shape, len(shape) - 1) < HEAD_DIM


def _row_is_low(shape):
    return lax.broadcasted_iota(jnp.int32, shape, 0) < HEAD_DIM


def _stack_heads_t(qt):
    low = _row_is_low(qt.shape)
    zero = jnp.zeros_like(qt)
    return jnp.concatenate([jnp.where(low, qt, zero), jnp.where(low, zero, qt)], axis=1)


def _rope_tile(y, c, s):
    lane = lax.broadcasted_iota(jnp.int32, y.shape, 1)
    first = (lane & (HEAD_DIM - 1)) < HEAD_DIM // 2
    swapped = jnp.where(first, pltpu.roll(y, LANES - HEAD_DIM // 2, 1), pltpu.roll(y, HEAD_DIM // 2, 1))
    return y * c + swapped * s


class _SoftmaxT:
    def __init__(self, m, l, acc):
        self.m, self.l, self.acc = m, l, acc

    @staticmethod
    def first(s, vt):
        m = jnp.max(s, axis=0, keepdims=True)
        p = jnp.exp2(s - m)
        return _SoftmaxT(m, jnp.sum(p, axis=0, keepdims=True), _dot(vt, p.astype(vt.dtype)))

    def update(self, s, vt):
        m_new = jnp.maximum(self.m, jnp.max(s, axis=0, keepdims=True))
        a = jnp.exp2(self.m - m_new)
        p = jnp.exp2(s - m_new)
        return _SoftmaxT(m_new, a * self.l + jnp.sum(p, axis=0, keepdims=True),
                         a * self.acc + _dot(vt, p.astype(vt.dtype)))


def _attend_chunks(st, qt, k_ref, vt_ref, tk, s_a, s_b):
    nk = k_ref.shape[0] // tk
    assert nk % 2 == 0

    def chunk(j):
        return pl.ds(pl.multiple_of(j * tk, tk), tk)

    s_a[...] = _dot(k_ref[0:tk, :], qt)

    def pair(i, carry):
        st = _SoftmaxT(*carry)
        j0 = 2 * i
        s_b[...] = _dot(k_ref[chunk(j0 + 1), :], qt)
        st = st.update(s_a[...], vt_ref[:, chunk(j0)])
        s_a[...] = _dot(k_ref[chunk(jnp.minimum(j0 + 2, nk - 1)), :], qt)
        st = st.update(s_b[...], vt_ref[:, chunk(j0 + 1)])
        return st.m, st.l, st.acc

    return _SoftmaxT(*lax.fori_loop(0, nk // 2, pair, (st.m, st.l, st.acc)))


def _mod_body(c_ref, w_ref, b_ref, o_ref):
    c = c_ref[...]
    h = (c * jax.nn.sigmoid(c)).astype(BF16)
    o_ref[...] = _dot(h, w_ref[...].astype(BF16)) + b_ref[...]


def _mod_call(cond, mod_w, mod_b):
    depth, d, n = mod_w.shape
    tn = 1024
    return pl.pallas_call(
        _mod_body,
        grid=(depth, n // tn),
        in_specs=[
            pl.BlockSpec((MOD_ROWS, d), lambda l, j: (0, 0)),
            pl.BlockSpec((None, d, tn), lambda l, j: (l, 0, j)),
            pl.BlockSpec((None, 1, tn), lambda l, j: (l, 0, j)),
        ],
        out_specs=pl.BlockSpec((None, MOD_ROWS, tn), lambda l, j: (l, 0, j)),
        out_shape=jax.ShapeDtypeStruct((depth, MOD_ROWS, n), F32),
        compiler_params=_cparams(2),
        name="adaln_mod",
    )(cond, mod_w, mod_b.reshape(depth, 1, n))


def _ffn_body(x_ref, g_ref, mod_ref, win_ref, wout_ref, *rest, base, ff, chunk, final):
    if final:
        fg_ref, o_ref, act_ref = rest
    else:
        o_ref, act_ref = rest
    x = x_ref[...]
    hb = _norm_mod(x, g_ref[...], mod_ref, base).astype(BF16)
    for c0 in range(0, ff, chunk):
        g = _dot(hb, win_ref[:, c0:c0 + chunk])
        u = _dot(hb, win_ref[:, ff + c0:ff + c0 + chunk])
        act_ref[:, c0:c0 + chunk] = (g * jax.nn.sigmoid(g) * u).astype(BF16)
    y = x + (0.5 * mod_ref[base + 2:base + 3, :]) * _dot(act_ref[...], wout_ref[...])
    if final:
        y = _rms(y) * fg_ref[...]
    o_ref[...] = y


def _ffn(x, g, mods_l, mod_row, w_in, w_out, base, final_g=None):
    bx, n, d = x.shape
    ff = w_out.shape[0]
    tm = min(512, n)
    final = final_g is not None
    in_specs = [
        pl.BlockSpec((None, tm, d), lambda b, i: (b, i, 0)),
        pl.BlockSpec((1, d), lambda b, i: (0, 0)),
        pl.BlockSpec((None, N_MOD, d), lambda b, i: (mod_row(b), 0, 0)),
        _resident((d, 2 * ff), lambda b, i: (0, 0)),
        _resident((ff, d), lambda b, i: (0, 0)),
    ]
    args = [x, g.reshape(1, d), mods_l, w_in, w_out]
    if final:
        in_specs.append(pl.BlockSpec((1, d), lambda b, i: (0, 0)))
        args.append(final_g.reshape(1, d))
    return pl.pallas_call(
        functools.partial(_ffn_body, base=base, ff=ff, chunk=MXU_WIDTH, final=final),
        grid=(bx, n // tm),
        in_specs=in_specs,
        out_specs=pl.BlockSpec((None, tm, d), lambda b, i: (b, i, 0)),
        out_shape=jax.ShapeDtypeStruct(x.shape, F32),
        scratch_shapes=[pltpu.VMEM((tm, ff), BF16)],
        compiler_params=_cparams(2),
        name="ffn_swiglu",
    )(*args)


PROJ_T_ROWS = 512


def _proj_body(*refs, base, use_mod, kinds, kinds_t):
    it = iter(refs)
    x_ref, g_ref = next(it), next(it)
    mod_ref = next(it) if use_mod else None
    w_ref = next(it) if kinds else None
    cs_ref = next(it) if any(k is not None for k in kinds) else None
    wt_ref = next(it) if kinds_t else None
    cst_ref = next(it) if any(rope for rope, _ in kinds_t) else None
    o_ref = next(it) if kinds else None
    ot_ref = next(it) if kinds_t else None
    hb = _norm_mod(x_ref[...].astype(F32), g_ref[...], mod_ref, base).astype(BF16)

    n_tiles = len(kinds)
    t = 0
    while t < n_tiles:
        w = min(MXU_WIDTH // LANES, n_tiles - t)
        y = _dot(hb, w_ref[:, t * LANES:(t + w) * LANES])
        for j in range(w):
            yj = y[:, j * LANES:(j + 1) * LANES]
            if kinds[t + j] is not None:
                yj = _rope_tile(yj, cs_ref[0], cs_ref[1])
            o_ref[:, (t + j) * LANES:(t + j + 1) * LANES] = yj.astype(o_ref.dtype)
        t += w

    half = HEAD_DIM // 2
    for r0 in range(0, len(kinds_t) * HEAD_DIM, PROJ_T_ROWS):
        yt = _dot_nt(wt_ref[r0:r0 + PROJ_T_ROWS, :], hb)
        for gi in range(PROJ_T_ROWS // HEAD_DIM):
            rope, scale = kinds_t[r0 // HEAD_DIM + gi]
            tile = yt[gi * HEAD_DIM:(gi + 1) * HEAD_DIM]
            if rope:
                swapped = jnp.concatenate([tile[half:], tile[:half]], axis=0)
                tile = tile * cst_ref[0] + swapped * cst_ref[1]
            if scale != 1.0:
                tile = tile * scale
            ot_ref[r0 + gi * HEAD_DIM:r0 + (gi + 1) * HEAD_DIM, :] = tile.astype(ot_ref.dtype)


def _proj(x, in_width, in_block, g, mods_l, mod_row, base, w=None, kinds=(), cs=None, out_dtype=BF16,
          wt=None, kinds_t=(), cst=None):
    bx, n, _ = x.shape
    tm = min(512, n)
    use_mod = mods_l is not None
    kinds, kinds_t = tuple(kinds), tuple(kinds_t)
    in_specs = [
        pl.BlockSpec((None, tm, in_width), lambda b, i: (b, i, in_block)),
        pl.BlockSpec((1, in_width), lambda b, i: (0, 0)),
    ]
    args = [x, g.reshape(1, in_width)]
    out_specs, out_shape = [], []
    if use_mod:
        in_specs.append(pl.BlockSpec((None, N_MOD, in_width), lambda b, i: (mod_row(b), 0, 0)))
        args.append(mods_l)
    if kinds:
        f = w.shape[1]
        assert f == len(kinds) * LANES
        in_specs.append(_resident((in_width, f), lambda b, i: (0, 0)))
        args.append(w)
        if any(k is not None for k in kinds):
            in_specs.append(pl.BlockSpec((2, tm, LANES), lambda b, i: (0, i, 0)))
            args.append(cs)
        out_specs.append(pl.BlockSpec((None, tm, f), lambda b, i: (b, i, 0)))
        out_shape.append(jax.ShapeDtypeStruct((bx, n, f), out_dtype))
    if kinds_t:
        ft = wt.shape[0]
        assert ft == len(kinds_t) * HEAD_DIM and ft % PROJ_T_ROWS == 0
        in_specs.append(_resident((ft, in_width), lambda b, i: (0, 0)))
        args.append(wt)
        if any(rope for rope, _ in kinds_t):
            in_specs.append(pl.BlockSpec((2, HEAD_DIM, tm), lambda b, i: (0, 0, i)))
            args.append(cst)
        out_specs.append(pl.BlockSpec((None, ft, tm), lambda b, i: (b, 0, i)))
        out_shape.append(jax.ShapeDtypeStruct((bx, ft, n), BF16))
    outs = pl.pallas_call(
        functools.partial(_proj_body, base=base, use_mod=use_mod, kinds=kinds, kinds_t=kinds_t),
        grid=(bx, n // tm),
        in_specs=in_specs,
        out_specs=out_specs,
        out_shape=out_shape,
        compiler_params=_cparams(2),
        name="norm_proj",
    )(*args)
    return outs[0] if len(outs) == 1 else outs


def _oproj_body(x_ref, a_ref, w_ref, mod_ref, o_ref, *, base):
    o_ref[...] = x_ref[...] + mod_ref[base:base + 1, :] * _dot(a_ref[...], w_ref[...])


def _oproj(x, a, w, mods_l, mod_row, base):
    bx, n, d = x.shape
    k = a.shape[-1]
    tm = min(512, n)
    return pl.pallas_call(
        functools.partial(_oproj_body, base=base),
        grid=(bx, n // tm),
        in_specs=[
            pl.BlockSpec((None, tm, d), lambda b, i: (b, i, 0)),
            pl.BlockSpec((None, tm, k), lambda b, i: (b, i, 0)),
            _resident((k, d), lambda b, i: (0, 0)),
            pl.BlockSpec((None, N_MOD, d), lambda b, i: (mod_row(b), 0, 0)),
        ],
        out_specs=pl.BlockSpec((None, tm, d), lambda b, i: (b, i, 0)),
        out_shape=jax.ShapeDtypeStruct(x.shape, F32),
        compiler_params=_cparams(2),
        name="out_proj_residual",
    )(x, a, w, mods_l)


def _lat_rows(blk):
    return lambda b, h, i: (b, 0, blk(h))


def _lat_t(blk):
    return lambda b, h, i: (b, blk(h), 0)


def _ctx_t(blk):
    return lambda b, h, i: (0, blk(h), b)


def _mla_body(*refs, tk, has_lat):
    if has_lat:
        qt_ref, kn_ref, kr_ref, vt_ref, knc_ref, krc_ref, vtc_ref, o_ref, kcat_ref, s_a, s_b = refs
    else:
        qt_ref, knc_ref, krc_ref, vtc_ref, o_ref = refs
    qt = qt_ref[...]
    kc = jnp.concatenate([knc_ref[...], krc_ref[...].astype(BF16)], axis=1)
    st = _SoftmaxT.first(_dot(kc, qt), vtc_ref[...])
    if has_lat:
        @pl.when(pl.program_id(2) == 0)
        def _():
            kcat_ref[:, :LANES] = kn_ref[...]
            kcat_ref[:, LANES:] = kr_ref[...].astype(BF16)

        st = _attend_chunks(st, qt, kcat_ref, vt_ref, tk, s_a, s_b)
    o_ref[...] = (st.acc / st.l).T.astype(o_ref.dtype)


def _mla_attn(qt, kvc, p1c, vtc, kv=None, p1=None, vt=None):
    h = MLA_HEADS
    has_lat = kv is not None
    bsz, c = kvc.shape[0], kvc.shape[1]
    lq = qt.shape[2] if has_lat else c
    tq = min(512, lq)
    rope_tile = (MLA_Q_RANK + MLA_KV_RANK) // LANES
    head = lambda hh: hh
    in_specs, args, scratch, tk = [], [qt], [], 0
    if has_lat:
        lk = kv.shape[1]
        tk = min(512, lk)
        in_specs += [
            pl.BlockSpec((None, 2 * LANES, tq), lambda b, hh, i: (b, hh, i)),
            pl.BlockSpec((None, lk, LANES), _lat_rows(head)),
            pl.BlockSpec((None, lk, LANES), _lat_rows(lambda hh: rope_tile)),
            pl.BlockSpec((None, LANES, lk), _lat_t(head)),
        ]
        args += [kv, p1, vt]
        scratch = [pltpu.VMEM((lk, 2 * LANES), BF16), pltpu.VMEM((tk, tq), F32), pltpu.VMEM((tk, tq), F32)]
    else:
        in_specs.append(pl.BlockSpec((None, 2 * LANES, tq), _ctx_t(head)))
    in_specs += [
        pl.BlockSpec((None, c, LANES), _lat_rows(head)),
        pl.BlockSpec((None, c, LANES), _lat_rows(lambda hh: rope_tile)),
        pl.BlockSpec((None, LANES, c), _ctx_t(head)),
    ]
    args += [kvc, p1c, vtc]
    return pl.pallas_call(
        functools.partial(_mla_body, tk=tk, has_lat=has_lat),
        grid=(bsz, h, lq // tq),
        in_specs=in_specs,
        out_specs=pl.BlockSpec((None, tq, LANES), lambda b, hh, i: (b, i, hh)),
        out_shape=jax.ShapeDtypeStruct((bsz, lq, h * MLA_V), BF16),
        scratch_shapes=scratch,
        compiler_params=_cparams(3),
        name="mla_attention",
    )(*args)


def _diff_body(*refs, lam_init, tk, has_lat):
    if has_lat:
        qt_ref, k_ref, vt_ref, kc_ref, vtc_ref, lam_ref, ng_ref, o_ref, s_a, s_b = refs
    else:
        qt_ref, kc_ref, vtc_ref, lam_ref, ng_ref, o_ref = refs
    tq = qt_ref.shape[1]
    q_plain = qt_ref[LANES:, :] if has_lat else qt_ref[...]
    st = _SoftmaxT.first(_dot(kc_ref[...], _stack_heads_t(q_plain)), vtc_ref[...])
    if has_lat:
        st = _attend_chunks(st, _stack_heads_t(qt_ref[:LANES, :]), k_ref, vt_ref, tk, s_a, s_b)
    lp = lam_ref[...]
    lam = (jnp.exp(jnp.sum(lp[0:1] * lp[1:2], axis=-1, keepdims=True))
           - jnp.exp(jnp.sum(lp[2:3] * lp[3:4], axis=-1, keepdims=True)) + lam_init)
    o = st.acc / st.l
    o = (o[:, :tq] - lam * o[:, tq:]).T
    o_ref[...] = (_rms(o) * ng_ref[...] * (1.0 - lam_init)).astype(o_ref.dtype)


def _diff_attn(kc, ptc, lam_params, norm_g, lam_init, qtc_blk, vtc_blk, kx=None, ptx=None):
    h = DIFF_HEADS
    has_lat = kx is not None
    bsz, c = kc.shape[0], kc.shape[1]
    lq = kx.shape[1] if has_lat else c
    tq = min(256, lq)
    head = lambda hh: hh
    in_specs, args, tk = [], [], 0
    if has_lat:
        lk = kx.shape[1]
        tk = min(512, lk)
        in_specs += [
            pl.BlockSpec((None, 2 * LANES, tq), lambda b, hh, i: (b, hh, i)),
            pl.BlockSpec((None, lk, LANES), _lat_rows(head)),
            pl.BlockSpec((None, LANES, lk), _lat_t(lambda hh: 2 * h + hh)),
        ]
        args += [ptx, kx, ptx]
    else:
        in_specs.append(pl.BlockSpec((None, LANES, tq), _ctx_t(lambda hh: qtc_blk + hh)))
        args.append(ptc)
    in_specs += [
        pl.BlockSpec((None, c, LANES), _lat_rows(head)),
        pl.BlockSpec((None, LANES, c), _ctx_t(lambda hh: vtc_blk + hh)),
        pl.BlockSpec((4, HEAD_DIM), lambda b, hh, i: (0, 0)),
        pl.BlockSpec((1, 2 * HEAD_DIM), lambda b, hh, i: (0, 0)),
    ]
    args += [kc, ptc, lam_params, norm_g.reshape(1, 2 * HEAD_DIM)]
    return pl.pallas_call(
        functools.partial(_diff_body, lam_init=lam_init, tk=tk, has_lat=has_lat),
        grid=(bsz, h, lq // tq),
        in_specs=in_specs,
        out_specs=pl.BlockSpec((None, tq, LANES), lambda b, hh, i: (b, i, hh)),
        out_shape=jax.ShapeDtypeStruct((bsz, lq, h * 2 * HEAD_DIM), BF16),
        scratch_shapes=[pltpu.VMEM((tk, 2 * tq), F32)] * 2 if has_lat else [],
        compiler_params=_cparams(3),
        name="diff_attention",
    )(*args)


def _swa_body(*refs, tq, has_lat):
    if has_lat:
        sink_ref, qr_ref, qp_ref, k_ref, v_ref, kc_ref, vc_ref, o_ref = refs
    else:
        sink_ref, qp_ref, kc_ref, vc_ref, o_ref = refs
    scale = HEAD_DIM ** -0.5
    pair = pl.program_id(1)
    kc, vc = kc_ref[...], vc_ref[...]
    if has_lat:
        lk = k_ref.shape[0]
        span = tq + 2 * SWA_WINDOW
        q0 = pl.program_id(2) * tq
        start = pl.multiple_of(jnp.clip(q0 - SWA_WINDOW, 0, lk - span), SWA_WINDOW)
        kw = k_ref[pl.ds(start, span), :]
        vw = v_ref[pl.ds(start, span), :]
        qpos = q0 + lax.broadcasted_iota(jnp.int32, (tq, span), 0)
        kpos = start + lax.broadcasted_iota(jnp.int32, (tq, span), 1)
        valid = jnp.abs(kpos - qpos) <= SWA_WINDOW
    low = _lane_is_low((tq, LANES))
    for g in range(SWA_GROUP):
        cols = slice(g * LANES, (g + 1) * LANES)
        qp = qp_ref[:, cols]
        zero = jnp.zeros_like(qp)
        outs = []
        for e in range(2):
            sink = sink_ref[(2 * pair + e) * SWA_GROUP + g]
            keep = low if e == 0 else jnp.logical_not(low)
            s_c = _dot_nt(jnp.where(keep, qp, zero), kc) * scale
            m = jnp.maximum(jnp.max(s_c, axis=-1, keepdims=True), sink)
            if has_lat:
                s_w = _dot_nt(jnp.where(keep, qr_ref[:, cols], zero), kw) * scale
                s_w = jnp.where(valid, s_w, NEG_INF)
                m = jnp.maximum(m, jnp.max(s_w, axis=-1, keepdims=True))
            p_c = jnp.exp(s_c - m)
            l = jnp.sum(p_c, axis=-1, keepdims=True) + jnp.exp(sink - m)
            o = _dot(p_c.astype(BF16), vc)
            if has_lat:
                p_w = jnp.exp(s_w - m)
                l = l + jnp.sum(p_w, axis=-1, keepdims=True)
                o = o + _dot(p_w.astype(BF16), vw)
            outs.append(o / l)
        o_ref[:, cols] = jnp.where(low, outs[0], outs[1]).astype(o_ref.dtype)


def _swa_attn(sink, pc, q_col_c, k_col_c, v_col_c, px=None):
    n_pairs = SWA_KV_HEADS // 2
    qw = SWA_GROUP * LANES
    has_lat = px is not None
    src = px if has_lat else pc
    b, lq, _ = src.shape
    c = pc.shape[1]
    tq = min(256, lq)
    in_specs = [pl.BlockSpec(memory_space=pltpu.SMEM)]
    args = [sink]
    if has_lat:
        lk = px.shape[1]
        q_tiles = n_pairs * SWA_GROUP
        in_specs += [
            pl.BlockSpec((None, tq, qw), lambda bi, p, i: (bi, i, p)),
            pl.BlockSpec((None, tq, qw), lambda bi, p, i: (bi, i, n_pairs + p)),
            pl.BlockSpec((None, lk, LANES), lambda bi, p, i: (bi, 0, 2 * q_tiles + p)),
            pl.BlockSpec((None, lk, LANES), lambda bi, p, i: (bi, 0, 2 * q_tiles + n_pairs + p)),
        ]
        args += [px, px, px, px]
    else:
        in_specs.append(pl.BlockSpec((None, tq, qw), lambda bi, p, i: (bi, i, q_col_c // SWA_GROUP + p)))
        args.append(pc)
    in_specs += [
        pl.BlockSpec((None, c, LANES), lambda bi, p, i: (bi, 0, k_col_c + p)),
        pl.BlockSpec((None, c, LANES), lambda bi, p, i: (bi, 0, v_col_c + p)),
    ]
    args += [pc, pc]
    return pl.pallas_call(
        functools.partial(_swa_body, tq=tq, has_lat=has_lat),
        grid=(b, n_pairs, lq // tq),
        in_specs=in_specs,
        out_specs=pl.BlockSpec((None, tq, qw), lambda bi, p, i: (bi, i, p)),
        out_shape=jax.ShapeDtypeStruct((b, lq, n_pairs * qw), BF16),
        compiler_params=_cparams(3),
        name="swa_attention",
    )(*args)


def _na_body(*refs, n_rows, has_lat):
    if has_lat:
        qt_ref, k_ref, vt_ref, bias_ref, kc_ref, vtc_ref, o_ref, s_a, s_b = refs
    else:
        qt_ref, kc_ref, vtc_ref, o_ref, s_a, s_b = refs
    bufs = (s_a, s_b)
    kc, vtc = kc_ref[...], vtc_ref[...]
    c = kc.shape[0]
    n_steps = qt_ref.shape[1] // LANES
    win = NA_WIN_ROWS * GRID_W
    low = _row_is_low((LANES, LANES))

    def scores(u):
        q = _stack_heads_t(qt_ref[:, u * LANES:(u + 1) * LANES])
        buf = bufs[u % 2]
        buf[0:c, :] = _dot(kc, q)
        if not has_lat:
            return None
        r = (pl.program_id(2) * n_steps + u) * NA_STEP_ROWS
        ws = jnp.clip(r - NA_KH // 2, 0, n_rows - NA_WIN_ROWS)
        cls = lax.shift_right_logical(r - ws, 1)
        keys = pl.ds(pl.multiple_of(ws * GRID_W, LANES), win)
        buf[c:c + win, :] = _dot(k_ref[keys, :], q) + bias_ref[cls]
        return keys

    keys = scores(0)
    for u in range(n_steps):
        next_keys = scores(u + 1) if u + 1 < n_steps else None
        s = bufs[u % 2][...]
        p = jnp.exp2(s - jnp.max(s, axis=0, keepdims=True))
        l = jnp.sum(p, axis=0, keepdims=True)
        p = p.astype(BF16)
        o = _dot(vtc, p[0:c])
        if has_lat:
            o = o + _dot(vt_ref[:, keys], p[c:])
        o = o / l
        o_ref[u * LANES:(u + 1) * LANES, :] = jnp.where(low, o[:, :LANES], o[:, LANES:]).T.astype(o_ref.dtype)
        keys = next_keys


def _na_attn(kc, ptc, bias=None, kx=None, ptx=None):
    n_pairs = NA_HEADS // 2
    has_lat = kx is not None
    bsz, c = kc.shape[0], kc.shape[1]
    pair = lambda p: p
    in_specs, args, n_rows = [], [], 0
    if has_lat:
        lq = kx.shape[1]
        n_rows = lq // GRID_W
        tq = min(512, lq)
        in_specs += [
            pl.BlockSpec((None, LANES, tq), lambda b, p, i: (b, p, i)),
            pl.BlockSpec((None, lq, LANES), _lat_rows(pair)),
            pl.BlockSpec((None, LANES, lq), _lat_t(lambda p: n_pairs + p)),
            pl.BlockSpec((None, NA_CLASSES, NA_WIN_ROWS * GRID_W, 2 * LANES), lambda b, p, i: (p, 0, 0, 0)),
        ]
        args += [ptx, kx, ptx, bias]
    else:
        lq = tq = c
        in_specs.append(pl.BlockSpec((None, LANES, tq), _ctx_t(pair)))
        args.append(ptc)
    in_specs += [
        pl.BlockSpec((None, c, LANES), _lat_rows(pair)),
        pl.BlockSpec((None, LANES, c), _ctx_t(lambda p: n_pairs + p)),
    ]
    args += [kc, ptc]
    return pl.pallas_call(
        functools.partial(_na_body, n_rows=n_rows, has_lat=has_lat),
        grid=(bsz, n_pairs, lq // tq),
        in_specs=in_specs,
        out_specs=pl.BlockSpec((None, tq, LANES), lambda b, p, i: (b, i, p)),
        out_shape=jax.ShapeDtypeStruct((bsz, lq, NA_HEADS * HEAD_DIM), BF16),
        scratch_shapes=[pltpu.VMEM((c + (NA_WIN_ROWS * GRID_W if has_lat else 0), 2 * LANES), F32)] * 2,
        compiler_params=_cparams(3),
        name="na_attention",
    )(*args)


def _rope_tables(seq):
    t = jnp.arange(seq, dtype=jnp.int32)
    row = (t // GRID_W).astype(F32)
    col = (t % GRID_W).astype(F32)
    n_freq = ROPE_DIM // 4
    inv = jnp.exp(-math.log(ROPE_BASE) * jnp.arange(n_freq, dtype=F32) / n_freq)
    ang = jnp.concatenate([row[:, None] * inv, col[:, None] * inv], axis=-1)
    cos, sin = jnp.cos(ang), jnp.sin(ang)
    c_head = jnp.concatenate([cos, cos], axis=-1)
    s_head = jnp.concatenate([-sin, sin], axis=-1)
    cs = jnp.stack([jnp.concatenate([c_head, c_head], axis=-1), jnp.concatenate([s_head, s_head], axis=-1)])
    cst = jnp.stack([c_head.T, s_head.T])
    return cs, cst


def _na_bias_table(rpb, n_rows):
    assert n_rows % 2 == 0 and n_rows >= NA_WIN_ROWS + 2
    rel_start = np.array([[0, 0], [0, 0], [0, 1], [2, 2], [2, 2]])
    dq = np.array([[0, 1], [2, 3], [4, 5], [6, 7], [8, 9]])
    jr = np.arange(NA_WIN_ROWS)
    row_ok = (jr >= rel_start[:, :, None]) & (jr < rel_start[:, :, None] + NA_KH)
    ridx = np.clip(jr - dq[:, :, None] + (NA_KH - 1), 0, 2 * NA_KH - 2)
    col = np.arange(GRID_W)
    col_start = np.clip(col - NA_KW // 2, 0, GRID_W - NA_KW)
    in_win = (col[None, :] >= col_start[:, None]) & (col[None, :] < col_start[:, None] + NA_KW)
    pad = GRID_W - NA_KW
    rows = jnp.pad(rpb.astype(F32)[:, ridx] * LOG2E, [(0, 0)] * 4 + [(pad, pad)])
    t = jnp.stack([rows[..., GRID_W - 1 - cq:2 * GRID_W - 1 - cq] for cq in range(GRID_W)], axis=-2)
    ok = row_ok[None, :, :, :, None, None] & in_win[None, None, None, None, :, :]
    t = jnp.where(ok, t, NEG_INF)
    n_pairs = rpb.shape[0] // 2
    t = t.reshape(n_pairs, 2, NA_CLASSES, NA_STEP_ROWS, NA_WIN_ROWS, GRID_W, GRID_W)
    t = t.transpose(0, 2, 4, 6, 1, 3, 5)
    return t.reshape(n_pairs, NA_CLASSES, NA_WIN_ROWS * GRID_W, 2 * LANES)


def _swa_q_perm():
    idx = []
    for pair in range(SWA_KV_HEADS // 2):
        for g in range(SWA_GROUP):
            for e in range(2):
                head = (2 * pair + e) * SWA_GROUP + g
                idx.extend(range(head * HEAD_DIM, (head + 1) * HEAD_DIM))
    return np.asarray(idx)


def kernel(x, c, ctx, c_ctx, mod_w, mod_b, norm_g, final_norm_g, ffn_w_in, ffn_w_out,
           mla_w_down, mla_q_norm, mla_kv_norm, mla_w_uq, mla_w_ukv, mla_w_o,
           swa_w_qkv, swa_sink, swa_w_o, na_w_qkv, na_rpb, na_w_o,
           diff_w_qkv, diff_lambda, diff_norm_g, diff_w_o):
    bsz, seq, d = x.shape
    n_ctx = ctx.shape[1]
    depth = mod_w.shape[0]
    assert bsz + 1 <= MOD_ROWS and seq % GRID_W == 0 and n_ctx % LANES == 0

    cond = jnp.zeros((MOD_ROWS, d), F32).at[:bsz].set(c).at[bsz].set(c_ctx)
    mods = _mod_call(cond, mod_w, mod_b).reshape(depth, MOD_ROWS, N_MOD, d)
    cs, cst = _rope_tables(seq)
    xc = ctx.reshape(1, bsz * n_ctx, d)
    x_row = lambda b: b
    c_row = lambda b: bsz
    c64 = HEAD_DIM ** -0.5 * LOG2E

    def per_batch(t):
        return t.reshape(bsz, n_ctx, t.shape[-1])

    for layer in range(depth):
        kind = layer % 4
        j = layer // 4
        last = layer == depth - 1
        ml = mods[layer]
        w_in = ffn_w_in[layer].astype(BF16)
        w_out = ffn_w_out[layer].astype(BF16)

        x = _ffn(x, norm_g[layer, 0], ml, x_row, w_in[0], w_out[0], 0)
        xc = _ffn(xc, norm_g[layer, 0], ml, c_row, w_in[0], w_out[0], 0)
        g1 = norm_g[layer, 1]

        if kind == 0:
            h = MLA_HEADS
            qk = MLA_Q_RANK + MLA_KV_RANK
            c_mla = (MLA_NOPE + ROPE_DIM) ** -0.5 * LOG2E
            wd = mla_w_down[j].astype(BF16)
            z = jnp.zeros((d, ROPE_DIM), BF16)
            w1x = jnp.concatenate([wd, z], axis=1)
            w1c = jnp.concatenate([wd[:, :qk], z, wd[:, qk:]], axis=1)
            n1 = w1x.shape[1] // LANES
            p1x = _proj(x, d, 0, g1, ml, x_row, 3, w=w1x, kinds=[None] * (n1 - 1) + [True], cs=cs, out_dtype=F32)
            p1c = _proj(xc, d, 0, g1, ml, c_row, 3, w=w1c, kinds=[None] * n1, out_dtype=F32)
            wq = mla_w_uq[j].astype(BF16).T.reshape(h, MLA_NOPE + ROPE_DIM, MLA_Q_RANK)
            wqt = jnp.concatenate([wq, wq[:, MLA_NOPE:]], axis=1).reshape(-1, MLA_Q_RANK)
            wkv = mla_w_ukv[j].astype(BF16).reshape(MLA_KV_RANK, h, MLA_NOPE + MLA_V)
            wk = wkv[:, :, :MLA_NOPE].reshape(MLA_KV_RANK, -1)
            wvt = wkv[:, :, MLA_NOPE:].reshape(MLA_KV_RANK, -1).T
            kv_blk = MLA_Q_RANK // MLA_KV_RANK
            q_plain = [(False, c_mla)] * 4
            q_kinds = ([(False, c_mla)] * 2 + [(True, c_mla), (False, c_mla)]) * h
            v_kinds = [(False, 1.0)] * (2 * h)
            k_none = [None] * h
            qtx = _proj(p1x, MLA_Q_RANK, 0, mla_q_norm[j], None, None, 0, wt=wqt, kinds_t=q_kinds, cst=cst)
            kvx, vtx = _proj(p1x, MLA_KV_RANK, kv_blk, mla_kv_norm[j], None, None, 0, w=wk, kinds=k_none,
                             wt=wvt, kinds_t=v_kinds)
            kvc, vtc = _proj(p1c, MLA_KV_RANK, kv_blk, mla_kv_norm[j], None, None, 0, w=wk, kinds=k_none,
                             wt=wvt, kinds_t=v_kinds)
            kvc, p1c_b = per_batch(kvc), per_batch(p1c)
            ox = _mla_attn(qtx, kvc, p1c_b, vtc, kvx, p1x, vtx)
            w_o = mla_w_o[j].astype(BF16)
            if not last:
                qtc = _proj(p1c, MLA_Q_RANK, 0, mla_q_norm[j], None, None, 0, wt=wqt, kinds_t=q_plain * h)
                oc = _mla_attn(qtc, kvc, p1c_b, vtc)
        elif kind == 1:
            wqkv = swa_w_qkv[j].astype(BF16)
            nq = SWA_KV_HEADS * SWA_GROUP * HEAD_DIM
            nk = SWA_KV_HEADS * HEAD_DIM
            perm = _swa_q_perm()
            wq = wqkv[:, :nq][:, perm]
            wk, wv = wqkv[:, nq:nq + nk], wqkv[:, nq + nk:]
            tq_, tk_ = nq // LANES, nk // LANES
            px = _proj(x, d, 0, g1, ml, x_row, 3, w=jnp.concatenate([wq, wq, wk, wv], axis=1),
                       kinds=[True] * tq_ + [None] * tq_ + [True] * tk_ + [None] * tk_, cs=cs)
            pc = per_batch(_proj(xc, d, 0, g1, ml, c_row, 3, w=jnp.concatenate([wq, wk, wv], axis=1),
                                 kinds=[None] * (tq_ + 2 * tk_)))
            ox = _swa_attn(swa_sink[j], pc, 0, tq_, tq_ + tk_, px)
            w_o = swa_w_o[j].astype(BF16)[perm, :]
            if not last:
                oc = _swa_attn(swa_sink[j], pc, 0, tq_, tq_ + tk_)
        elif kind == 2:
            wqkv = na_w_qkv[j].astype(BF16)
            nh = NA_HEADS * HEAD_DIM
            wk = wqkv[:, nh:2 * nh]
            wt = jnp.concatenate([wqkv[:, :nh], wqkv[:, 2 * nh:]], axis=1).T
            kt = [(False, c64)] * NA_HEADS + [(False, 1.0)] * NA_HEADS
            k_none = [None] * (nh // LANES)
            kx, ptx = _proj(x, d, 0, g1, ml, x_row, 3, w=wk, kinds=k_none, wt=wt, kinds_t=kt)
            kc, ptc = _proj(xc, d, 0, g1, ml, c_row, 3, w=wk, kinds=k_none, wt=wt, kinds_t=kt)
            kc = per_batch(kc)
            ox = _na_attn(kc, ptc, _na_bias_table(na_rpb[j], seq // GRID_W), kx, ptx)
            w_o = na_w_o[j].astype(BF16)
            if not last:
                oc = _na_attn(kc, ptc)
        else:
            h = DIFF_HEADS
            lam_init = 0.8 - 0.6 * math.exp(-0.3 * layer)
            wqkv = diff_w_qkv[j].astype(BF16)
            nq = 2 * h * HEAD_DIM
            wk = wqkv[:, nq:2 * nq]
            wqt = wqkv[:, :nq].T
            wvt = wqkv[:, 2 * nq:].T
            wq2 = wqt.reshape(h, 2 * HEAD_DIM, d)
            wtx = jnp.concatenate([jnp.concatenate([wq2, wq2], axis=1).reshape(-1, d), wvt], axis=0)
            ktx = ([(True, c64)] * 2 + [(False, c64)] * 2) * h + [(False, 1.0)] * (2 * h)
            ktc = [(False, c64)] * (2 * h) + [(False, 1.0)] * (2 * h)
            kx, ptx = _proj(x, d, 0, g1, ml, x_row, 3, w=wk, kinds=[True] * h, cs=cs, wt=wtx, kinds_t=ktx, cst=cst)
            kc, ptc = _proj(xc, d, 0, g1, ml, c_row, 3, w=wk, kinds=[None] * h,
                            wt=jnp.concatenate([wqt, wvt], axis=0), kinds_t=ktc)
            kc = per_batch(kc)
            ox = _diff_attn(kc, ptc, diff_lambda[j], diff_norm_g[j], lam_init, 0, h, kx, ptx)
            w_o = diff_w_o[j].astype(BF16)
            if not last:
                oc = _diff_attn(kc, ptc, diff_lambda[j], diff_norm_g[j], lam_init, 0, h)

        x = _oproj(x, ox, w_o, ml, x_row, 5)
        x = _ffn(x, norm_g[layer, 2], ml, x_row, w_in[1], w_out[1], 6,
                 final_g=final_norm_g if last else None)
        if not last:
            xc = _oproj(xc, oc.reshape(1, bsz * n_ctx, oc.shape[-1]), w_o, ml, c_row, 5)
            xc = _ffn(xc, norm_g[layer, 2], ml, c_row, w_in[1], w_out[1], 6)
    return x
```

```python
import functools
import math

import jax
import jax.numpy as jnp
import numpy as np
from jax import lax
from jax.experimental import pallas as pl
from jax.experimental.pallas import tpu as pltpu

F32 = jnp.float32
BF16 = jnp.bfloat16

GRID_W = 64
N_MOD = 9
EPS = 1e-6
ROPE_BASE = 10000.0
ROPE_DIM = 64
NEG_INF = -1e30
LOG2E = math.log2(math.e)

MLA_HEADS = 8
MLA_Q_RANK = 512
MLA_KV_RANK = 256
MLA_NOPE = 128
MLA_V = 128

SWA_KV_HEADS = 4
SWA_GROUP = 4
SWA_WINDOW = 128

NA_HEADS = 16
NA_KH = 8
NA_KW = 16
NA_STEP_ROWS = 2
NA_WIN_ROWS = NA_KH + NA_STEP_ROWS
NA_CLASSES = 5

DIFF_HEADS = 8
HEAD_DIM = 64

LANES = 128
MXU_WIDTH = 256
VMEM_LIMIT_BYTES = 56 * 1024 * 1024
MOD_ROWS = 8


def _cparams(n_axes):
    return pltpu.CompilerParams(
        dimension_semantics=("arbitrary",) * n_axes,
        vmem_limit_bytes=VMEM_LIMIT_BYTES,
    )


def _resident(shape, index_map):
    return pl.BlockSpec(shape, index_map, pipeline_mode=pl.Buffered(1))


def _rms(x):
    return x * lax.rsqrt(jnp.mean(x * x, axis=-1, keepdims=True) + EPS)


def _norm_mod(x, g, mod_ref, base):
    y = _rms(x) * g
    if mod_ref is not None:
        y = y * (1.0 + mod_ref[base + 1:base + 2, :]) + mod_ref[base:base + 1, :]
    return y


def _dot(a, b):
    return jnp.dot(a, b, preferred_element_type=F32)


def _dot_nt(a, b):
    return lax.dot_general(a, b, (((1,), (1,)), ((), ())), preferred_element_type=F32)


def _row_is_low(shape):
    return lax.broadcasted_iota(jnp.int32, shape, 0) < HEAD_DIM


def _stack_heads_t(qt):
    low = _row_is_low(qt.shape)
    zero = jnp.zeros_like(qt)
    return jnp.concatenate([jnp.where(low, qt, zero), jnp.where(low, zero, qt)], axis=1)


def _rope_tile(y, c, s):
    lane = lax.broadcasted_iota(jnp.int32, y.shape, 1)
    first = (lane & (HEAD_DIM - 1)) < HEAD_DIM // 2
    swapped = jnp.where(first, pltpu.roll(y, LANES - HEAD_DIM // 2, 1), pltpu.roll(y, HEAD_DIM // 2, 1))
    return y * c + swapped * s


class _SoftmaxT:
    def __init__(self, m, l, acc):
        self.m, self.l, self.acc = m, l, acc

    @staticmethod
    def first(s, vt):
        m = jnp.max(s, axis=0, keepdims=True)
        p = jnp.exp2(s - m)
        return _SoftmaxT(m, jnp.sum(p, axis=0, keepdims=True), _dot(vt, p.astype(vt.dtype)))

    def update(self, s, vt):
        m_new = jnp.maximum(self.m, jnp.max(s, axis=0, keepdims=True))
        a = jnp.exp2(self.m - m_new)
        p = jnp.exp2(s - m_new)
        return _SoftmaxT(m_new, a * self.l + jnp.sum(p, axis=0, keepdims=True),
                         a * self.acc + _dot(vt, p.astype(vt.dtype)))


def _attend(qt_ctx, kc, vtc, bufs, qt_lat=None, k_ref=None, vt_ref=None, tk=0):
    c = kc.shape[0]
    n = 1 + (k_ref.shape[0] // tk if k_ref is not None else 0)

    def write_scores(j):
        if j == 0:
            bufs[0][0:c, :] = _dot(kc, qt_ctx)
        else:
            bufs[j % 2][0:tk, :] = _dot(k_ref[(j - 1) * tk:j * tk, :], qt_lat)

    write_scores(0)
    st = None
    for j in range(n):
        if j + 1 < n:
            write_scores(j + 1)
        if j == 0:
            st = _SoftmaxT.first(bufs[0][0:c, :], vtc)
        else:
            st = st.update(bufs[j % 2][0:tk, :], vt_ref[:, (j - 1) * tk:j * tk])
    return st


def _mod_body(c_ref, w_ref, b_ref, o_ref):
    c = c_ref[...]
    h = (c * jax.nn.sigmoid(c)).astype(BF16)
    o_ref[...] = _dot(h, w_ref[...].astype(BF16)) + b_ref[...]


def _mod_call(cond, mod_w, mod_b):
    depth, d, n = mod_w.shape
    tn = 1024
    return pl.pallas_call(
        _mod_body,
        grid=(depth, n // tn),
        in_specs=[
            pl.BlockSpec((MOD_ROWS, d), lambda l, j: (0, 0)),
            pl.BlockSpec((None, d, tn), lambda l, j: (l, 0, j)),
            pl.BlockSpec((None, 1, tn), lambda l, j: (l, 0, j)),
        ],
        out_specs=pl.BlockSpec((None, MOD_ROWS, tn), lambda l, j: (l, 0, j)),
        out_shape=jax.ShapeDtypeStruct((depth, MOD_ROWS, n), F32),
        compiler_params=_cparams(2),
        name="adaln_mod",
    )(cond, mod_w, mod_b.reshape(depth, 1, n))


def _ffn_body(x_ref, g_ref, mod_ref, win_ref, wout_ref, *rest, base, ff, chunk, final):
    if final:
        fg_ref, o_ref, act_ref = rest
    else:
        o_ref, act_ref = rest
    x = x_ref[...]
    hb = _norm_mod(x, g_ref[...], mod_ref, base).astype(BF16)
    for c0 in range(0, ff, chunk):
        g = _dot(hb, win_ref[:, c0:c0 + chunk])
        u = _dot(hb, win_ref[:, ff + c0:ff + c0 + chunk])
        act_ref[:, c0:c0 + chunk] = (g * jax.nn.sigmoid(g) * u).astype(BF16)
    y = x + (0.5 * mod_ref[base + 2:base + 3, :]) * _dot(act_ref[...], wout_ref[...])
    if final:
        y = _rms(y) * fg_ref[...]
    o_ref[...] = y


def _ffn(x, g, mods_l, mod_row, w_in, w_out, base, final_g=None):
    bx, n, d = x.shape
    ff = w_out.shape[0]
    tm = min(512, n)
    final = final_g is not None
    in_specs = [
        pl.BlockSpec((None, tm, d), lambda b, i: (b, i, 0)),
        pl.BlockSpec((1, d), lambda b, i: (0, 0)),
        pl.BlockSpec((None, N_MOD, d), lambda b, i: (mod_row(b), 0, 0)),
        _resident((d, 2 * ff), lambda b, i: (0, 0)),
        _resident((ff, d), lambda b, i: (0, 0)),
    ]
    args = [x, g.reshape(1, d), mods_l, w_in, w_out]
    if final:
        in_specs.append(pl.BlockSpec((1, d), lambda b, i: (0, 0)))
        args.append(final_g.reshape(1, d))
    return pl.pallas_call(
        functools.partial(_ffn_body, base=base, ff=ff, chunk=MXU_WIDTH, final=final),
        grid=(bx, n // tm),
        in_specs=in_specs,
        out_specs=pl.BlockSpec((None, tm, d), lambda b, i: (b, i, 0)),
        out_shape=jax.ShapeDtypeStruct(x.shape, F32),
        scratch_shapes=[pltpu.VMEM((tm, ff), BF16)],
        compiler_params=_cparams(2),
        name="ffn_swiglu",
    )(*args)


PROJ_T_ROWS = 512


def _proj_body(*refs, base, use_mod, kinds, kinds_t):
    it = iter(refs)
    x_ref, g_ref = next(it), next(it)
    mod_ref = next(it) if use_mod else None
    w_ref = next(it) if kinds else None
    cs_ref = next(it) if any(k is not None for k in kinds) else None
    wt_ref = next(it) if kinds_t else None
    cst_ref = next(it) if any(rope for rope, _ in kinds_t) else None
    o_ref = next(it) if kinds else None
    ot_ref = next(it) if kinds_t else None
    hb = _norm_mod(x_ref[...].astype(F32), g_ref[...], mod_ref, base).astype(BF16)

    n_tiles = len(kinds)
    t = 0
    while t < n_tiles:
        w = min(MXU_WIDTH // LANES, n_tiles - t)
        y = _dot(hb, w_ref[:, t * LANES:(t + w) * LANES])
        for j in range(w):
            yj = y[:, j * LANES:(j + 1) * LANES]
            if kinds[t + j] is not None:
                yj = _rope_tile(yj, cs_ref[0], cs_ref[1])
            o_ref[:, (t + j) * LANES:(t + j + 1) * LANES] = yj.astype(o_ref.dtype)
        t += w

    half = HEAD_DIM // 2
    n_rows_t = len(kinds_t) * HEAD_DIM
    for r0 in range(0, n_rows_t, PROJ_T_ROWS):
        rows = min(PROJ_T_ROWS, n_rows_t - r0)
        yt = _dot_nt(wt_ref[r0:r0 + rows, :], hb)
        for gi in range(rows // HEAD_DIM):
            rope, scale = kinds_t[r0 // HEAD_DIM + gi]
            tile = yt[gi * HEAD_DIM:(gi + 1) * HEAD_DIM]
            if rope:
                swapped = jnp.concatenate([tile[half:], tile[:half]], axis=0)
                tile = tile * cst_ref[0] + swapped * cst_ref[1]
            if scale != 1.0:
                tile = tile * scale
            ot_ref[r0 + gi * HEAD_DIM:r0 + (gi + 1) * HEAD_DIM, :] = tile.astype(ot_ref.dtype)


def _proj(x, in_width, in_block, g, mods_l, mod_row, base, w=None, kinds=(), cs=None, out_dtype=BF16,
          wt=None, kinds_t=(), cst=None):
    bx, n, _ = x.shape
    tm = min(512, n)
    use_mod = mods_l is not None
    kinds, kinds_t = tuple(kinds), tuple(kinds_t)
    in_specs = [
        pl.BlockSpec((None, tm, in_width), lambda b, i: (b, i, in_block)),
        pl.BlockSpec((1, in_width), lambda b, i: (0, 0)),
    ]
    args = [x, g.reshape(1, in_width)]
    out_specs, out_shape = [], []
    if use_mod:
        in_specs.append(pl.BlockSpec((None, N_MOD, in_width), lambda b, i: (mod_row(b), 0, 0)))
        args.append(mods_l)
    if kinds:
        f = w.shape[1]
        assert f == len(kinds) * LANES
        in_specs.append(_resident((in_width, f), lambda b, i: (0, 0)))
        args.append(w)
        if any(k is not None for k in kinds):
            in_specs.append(pl.BlockSpec((2, tm, LANES), lambda b, i: (0, i, 0)))
            args.append(cs)
        out_specs.append(pl.BlockSpec((None, tm, f), lambda b, i: (b, i, 0)))
        out_shape.append(jax.ShapeDtypeStruct((bx, n, f), out_dtype))
    if kinds_t:
        ft = wt.shape[0]
        assert ft == len(kinds_t) * HEAD_DIM
        in_specs.append(_resident((ft, in_width), lambda b, i: (0, 0)))
        args.append(wt)
        if any(rope for rope, _ in kinds_t):
            in_specs.append(pl.BlockSpec((2, HEAD_DIM, tm), lambda b, i: (0, 0, i)))
            args.append(cst)
        out_specs.append(pl.BlockSpec((None, ft, tm), lambda b, i: (b, 0, i)))
        out_shape.append(jax.ShapeDtypeStruct((bx, ft, n), BF16))
    outs = pl.pallas_call(
        functools.partial(_proj_body, base=base, use_mod=use_mod, kinds=kinds, kinds_t=kinds_t),
        grid=(bx, n // tm),
        in_specs=in_specs,
        out_specs=out_specs,
        out_shape=out_shape,
        compiler_params=_cparams(2),
        name="norm_proj",
    )(*args)
    return outs[0] if len(outs) == 1 else outs


def _oproj_body(x_ref, a_ref, w_ref, mod_ref, o_ref, *, base):
    o_ref[...] = x_ref[...] + mod_ref[base:base + 1, :] * _dot(a_ref[...], w_ref[...])


def _oproj(x, a, w, mods_l, mod_row, base):
    bx, n, d = x.shape
    k = a.shape[-1]
    tm = min(512, n)
    return pl.pallas_call(
        functools.partial(_oproj_body, base=base),
        grid=(bx, n // tm),
        in_specs=[
            pl.BlockSpec((None, tm, d), lambda b, i: (b, i, 0)),
            pl.BlockSpec((None, tm, k), lambda b, i: (b, i, 0)),
            _resident((k, d), lambda b, i: (0, 0)),
            pl.BlockSpec((None, N_MOD, d), lambda b, i: (mod_row(b), 0, 0)),
        ],
        out_specs=pl.BlockSpec((None, tm, d), lambda b, i: (b, i, 0)),
        out_shape=jax.ShapeDtypeStruct(x.shape, F32),
        compiler_params=_cparams(2),
        name="out_proj_residual",
    )(x, a, w, mods_l)


def _lat_rows(blk):
    return lambda b, h, i: (b, 0, blk(h))


def _lat_t(blk):
    return lambda b, h, i: (b, blk(h), 0)


def _ctx_t(blk):
    return lambda b, h, i: (0, blk(h), b)


def _mla_body(*refs, tk, has_lat):
    if has_lat:
        qt_ref, kn_ref, kr_ref, vt_ref, knc_ref, krc_ref, vtc_ref, o_ref, s_a, s_b, kcat_ref = refs
    else:
        qt_ref, knc_ref, krc_ref, vtc_ref, o_ref, s_a, s_b = refs
        kcat_ref = vt_ref = None
    qt = qt_ref[...]
    kc = jnp.concatenate([knc_ref[...], krc_ref[...].astype(BF16)], axis=1)
    if has_lat:
        @pl.when(pl.program_id(2) == 0)
        def _():
            kcat_ref[:, :LANES] = kn_ref[...]
            kcat_ref[:, LANES:] = kr_ref[...].astype(BF16)

    st = _attend(qt, kc, vtc_ref[...], (s_a, s_b), qt, kcat_ref, vt_ref, tk)
    o_ref[...] = (st.acc / st.l).T.astype(o_ref.dtype)


def _mla_attn(qt, kvc, p1c, vtc, kv=None, p1=None, vt=None):
    h = MLA_HEADS
    has_lat = kv is not None
    bsz, c = kvc.shape[0], kvc.shape[1]
    lq = qt.shape[2] if has_lat else c
    tq = min(1024, lq)
    rope_tile = (MLA_Q_RANK + MLA_KV_RANK) // LANES
    head = lambda hh: hh
    in_specs, args, scratch, tk = [], [qt], [], 0
    if has_lat:
        lk = kv.shape[1]
        tk = min(512, lk)
        in_specs += [
            pl.BlockSpec((None, 2 * LANES, tq), lambda b, hh, i: (b, hh, i)),
            pl.BlockSpec((None, lk, LANES), _lat_rows(head)),
            pl.BlockSpec((None, lk, LANES), _lat_rows(lambda hh: rope_tile)),
            pl.BlockSpec((None, LANES, lk), _lat_t(head)),
        ]
        args += [kv, p1, vt]
        scratch = [pltpu.VMEM((lk, 2 * LANES), BF16)]
    else:
        in_specs.append(pl.BlockSpec((None, 2 * LANES, tq), _ctx_t(head)))
    in_specs += [
        pl.BlockSpec((None, c, LANES), _lat_rows(head)),
        pl.BlockSpec((None, c, LANES), _lat_rows(lambda hh: rope_tile)),
        pl.BlockSpec((None, LANES, c), _ctx_t(head)),
    ]
    args += [kvc, p1c, vtc]
    return pl.pallas_call(
        functools.partial(_mla_body, tk=tk, has_lat=has_lat),
        grid=(bsz, h, lq // tq),
        in_specs=in_specs,
        out_specs=pl.BlockSpec((None, tq, LANES), lambda b, hh, i: (b, i, hh)),
        out_shape=jax.ShapeDtypeStruct((bsz, lq, h * MLA_V), BF16),
        scratch_shapes=[pltpu.VMEM((max(tk, c), tq), F32)] * 2 + scratch,
        compiler_params=_cparams(3),
        name="mla_attention",
    )(*args)


def _diff_body(*refs, lam_init, tk, has_lat):
    if has_lat:
        qt_ref, k_ref, vt_ref, kc_ref, vtc_ref, lam_ref, ng_ref, o_ref, s_a, s_b = refs
        q_plain, q_rot = _stack_heads_t(qt_ref[LANES:, :]), _stack_heads_t(qt_ref[:LANES, :])
    else:
        qt_ref, kc_ref, vtc_ref, lam_ref, ng_ref, o_ref, s_a, s_b = refs
        q_plain, q_rot, k_ref, vt_ref = _stack_heads_t(qt_ref[...]), None, None, None
    tq = qt_ref.shape[1]
    st = _attend(q_plain, kc_ref[...], vtc_ref[...], (s_a, s_b), q_rot, k_ref, vt_ref, tk)
    lp = lam_ref[...]
    lam = (jnp.exp(jnp.sum(lp[0:1] * lp[1:2], axis=-1, keepdims=True))
           - jnp.exp(jnp.sum(lp[2:3] * lp[3:4], axis=-1, keepdims=True)) + lam_init)
    o = st.acc / st.l
    o = (o[:, :tq] - lam * o[:, tq:]).T
    o_ref[...] = (_rms(o) * ng_ref[...] * (1.0 - lam_init)).astype(o_ref.dtype)


def _diff_attn(kc, ptc, lam_params, norm_g, lam_init, qtc_blk, vtc_blk, kx=None, ptx=None):
    h = DIFF_HEADS
    has_lat = kx is not None
    bsz, c = kc.shape[0], kc.shape[1]
    lq = kx.shape[1] if has_lat else c
    tq = min(512, lq)
    head = lambda hh: hh
    in_specs, args, tk = [], [], 0
    if has_lat:
        lk = kx.shape[1]
        tk = min(512, lk)
        in_specs += [
            pl.BlockSpec((None, 2 * LANES, tq), lambda b, hh, i: (b, hh, i)),
            pl.BlockSpec((None, lk, LANES), _lat_rows(head)),
            pl.BlockSpec((None, LANES, lk), _lat_t(lambda hh: 2 * h + hh)),
        ]
        args += [ptx, kx, ptx]
    else:
        in_specs.append(pl.BlockSpec((None, LANES, tq), _ctx_t(lambda hh: qtc_blk + hh)))
        args.append(ptc)
    in_specs += [
        pl.BlockSpec((None, c, LANES), _lat_rows(head)),
        pl.BlockSpec((None, LANES, c), _ctx_t(lambda hh: vtc_blk + hh)),
        pl.BlockSpec((4, HEAD_DIM), lambda b, hh, i: (0, 0)),
        pl.BlockSpec((1, 2 * HEAD_DIM), lambda b, hh, i: (0, 0)),
    ]
    args += [kc, ptc, lam_params, norm_g.reshape(1, 2 * HEAD_DIM)]
    return pl.pallas_call(
        functools.partial(_diff_body, lam_init=lam_init, tk=tk, has_lat=has_lat),
        grid=(bsz, h, lq // tq),
        in_specs=in_specs,
        out_specs=pl.BlockSpec((None, tq, LANES), lambda b, hh, i: (b, i, hh)),
        out_shape=jax.ShapeDtypeStruct((bsz, lq, h * 2 * HEAD_DIM), BF16),
        scratch_shapes=[pltpu.VMEM((max(tk, c), 2 * tq), F32)] * 2,
        compiler_params=_cparams(3),
        name="diff_attention",
    )(*args)


def _swa_body(*refs, tq, has_lat):
    if has_lat:
        sink_ref, qr_ref, qp_ref, k_ref, vt_ref, kc_ref, vtc_ref, o_ref, s_a, s_b = refs
    else:
        sink_ref, qp_ref, kc_ref, vtc_ref, o_ref, s_a, s_b = refs
    bufs = (s_a, s_b)
    pair = pl.program_id(1)
    kc, vtc = kc_ref[...], vtc_ref[...]
    c = kc.shape[0]
    span = tq + 2 * SWA_WINDOW if has_lat else 0
    first_half = lax.broadcasted_iota(jnp.int32, (1, 2 * tq), 1) < tq
    if has_lat:
        lk = k_ref.shape[0]
        q0 = pl.program_id(2) * tq
        start = pl.multiple_of(jnp.clip(q0 - SWA_WINDOW, 0, lk - span), SWA_WINDOW)
        keys = pl.ds(start, span)
        kpos = start + lax.broadcasted_iota(jnp.int32, (span, 2 * tq), 0)
        qpos = q0 + (lax.broadcasted_iota(jnp.int32, (span, 2 * tq), 1) & (tq - 1))
        valid = jnp.abs(kpos - qpos) <= SWA_WINDOW
    low = _row_is_low((LANES, tq))

    def scores(g):
        rows = slice(g * LANES, (g + 1) * LANES)
        buf = bufs[g % 2]
        buf[0:c, :] = _dot(kc, _stack_heads_t(qp_ref[rows, :]))
        if has_lat:
            s_w = _dot(k_ref[keys, :], _stack_heads_t(qr_ref[rows, :]))
            buf[c:c + span, :] = jnp.where(valid, s_w, NEG_INF)

    scores(0)
    for g in range(SWA_GROUP):
        if g + 1 < SWA_GROUP:
            scores(g + 1)
        sink = jnp.where(first_half, sink_ref[2 * pair * SWA_GROUP + g],
                         sink_ref[(2 * pair + 1) * SWA_GROUP + g]) * LOG2E
        s = bufs[g % 2][...]
        m = jnp.maximum(jnp.max(s, axis=0, keepdims=True), sink)
        p = jnp.exp2(s - m)
        l = jnp.sum(p, axis=0, keepdims=True) + jnp.exp2(sink - m)
        p = p.astype(BF16)
        o = _dot(vtc, p[0:c])
        if has_lat:
            o = o + _dot(vt_ref[:, keys], p[c:])
        o = o / l
        o_ref[:, g * LANES:(g + 1) * LANES] = jnp.where(low, o[:, :tq], o[:, tq:]).T.astype(o_ref.dtype)


def _swa_attn(sink, kc, ptc, kx=None, ptx=None):
    n_pairs = SWA_KV_HEADS // 2
    qw = SWA_GROUP * LANES
    has_lat = kx is not None
    bsz, c = kc.shape[0], kc.shape[1]
    lq = kx.shape[1] if has_lat else c
    tq = min(256, lq)
    pair = lambda p: p
    in_specs = [pl.BlockSpec(memory_space=pltpu.SMEM)]
    args = [sink]
    span = 0
    if has_lat:
        lk = kx.shape[1]
        span = tq + 2 * SWA_WINDOW
        in_specs += [
            pl.BlockSpec((None, qw, tq), lambda b, p, i: (b, p, i)),
            pl.BlockSpec((None, qw, tq), lambda b, p, i: (b, n_pairs + p, i)),
            pl.BlockSpec((None, lk, LANES), _lat_rows(pair)),
            pl.BlockSpec((None, LANES, lk), _lat_t(lambda p: 2 * n_pairs * SWA_GROUP + p)),
        ]
        args += [ptx, ptx, kx, ptx]
    else:
        in_specs.append(pl.BlockSpec((None, qw, tq), _ctx_t(pair)))
        args.append(ptc)
    in_specs += [
        pl.BlockSpec((None, c, LANES), _lat_rows(pair)),
        pl.BlockSpec((None, LANES, c), _ctx_t(lambda p: n_pairs * SWA_GROUP + p)),
    ]
    args += [kc, ptc]
    return pl.pallas_call(
        functools.partial(_swa_body, tq=tq, has_lat=has_lat),
        grid=(bsz, n_pairs, lq // tq),
        in_specs=in_specs,
        out_specs=pl.BlockSpec((None, tq, qw), lambda b, p, i: (b, i, p)),
        out_shape=jax.ShapeDtypeStruct((bsz, lq, n_pairs * qw), BF16),
        scratch_shapes=[pltpu.VMEM((c + span, 2 * tq), F32)] * 2,
        compiler_params=_cparams(3),
        name="swa_attention",
    )(*args)


def _na_body(*refs, n_rows, has_lat):
    if has_lat:
        qt_ref, k_ref, vt_ref, bias_ref, kc_ref, vtc_ref, o_ref, s_a, s_b = refs
    else:
        qt_ref, kc_ref, vtc_ref, o_ref, s_a, s_b = refs
    bufs = (s_a, s_b)
    kc, vtc = kc_ref[...], vtc_ref[...]
    c = kc.shape[0]
    n_steps = qt_ref.shape[1] // LANES
    win = NA_WIN_ROWS * GRID_W
    low = _row_is_low((LANES, LANES))

    def scores(u):
        q = _stack_heads_t(qt_ref[:, u * LANES:(u + 1) * LANES])
        buf = bufs[u % 2]
        buf[0:c, :] = _dot(kc, q)
        if not has_lat:
            return None
        r = (pl.program_id(2) * n_steps + u) * NA_STEP_ROWS
        ws = jnp.clip(r - NA_KH // 2, 0, n_rows - NA_WIN_ROWS)
        cls = lax.shift_right_logical(r - ws, 1)
        keys = pl.ds(pl.multiple_of(ws * GRID_W, LANES), win)
        buf[c:c + win, :] = _dot(k_ref[keys, :], q) + bias_ref[cls]
        return keys

    keys = scores(0)
    for u in range(n_steps):
        next_keys = scores(u + 1) if u + 1 < n_steps else None
        s = bufs[u % 2][...]
        p = jnp.exp2(s - jnp.max(s, axis=0, keepdims=True))
        l = jnp.sum(p, axis=0, keepdims=True)
        p = p.astype(BF16)
        o = _dot(vtc, p[0:c])
        if has_lat:
            o = o + _dot(vt_ref[:, keys], p[c:])
        o = o / l
        o_ref[u * LANES:(u + 1) * LANES, :] = jnp.where(low, o[:, :LANES], o[:, LANES:]).T.astype(o_ref.dtype)
        keys = next_keys


def _na_attn(kc, ptc, bias=None, kx=None, ptx=None):
    n_pairs = NA_HEADS // 2
    has_lat = kx is not None
    bsz, c = kc.shape[0], kc.shape[1]
    pair = lambda p: p
    in_specs, args, n_rows = [], [], 0
    if has_lat:
        lq = kx.shape[1]
        n_rows = lq // GRID_W
        tq = min(512, lq)
        in_specs += [
            pl.BlockSpec((None, LANES, tq), lambda b, p, i: (b, p, i)),
            pl.BlockSpec((None, lq, LANES), _lat_rows(pair)),
            pl.BlockSpec((None, LANES, lq), _lat_t(lambda p: n_pairs + p)),
            pl.BlockSpec((None, NA_CLASSES, NA_WIN_ROWS * GRID_W, 2 * LANES), lambda b, p, i: (p, 0, 0, 0)),
        ]
        args += [ptx, kx, ptx, bias]
    else:
        lq = tq = c
        in_specs.append(pl.BlockSpec((None, LANES, tq), _ctx_t(pair)))
        args.append(ptc)
    in_specs += [
        pl.BlockSpec((None, c, LANES), _lat_rows(pair)),
        pl.BlockSpec((None, LANES, c), _ctx_t(lambda p: n_pairs + p)),
    ]
    args += [kc, ptc]
    return pl.pallas_call(
        functools.partial(_na_body, n_rows=n_rows, has_lat=has_lat),
        grid=(bsz, n_pairs, lq // tq),
        in_specs=in_specs,
        out_specs=pl.BlockSpec((None, tq, LANES), lambda b, p, i: (b, i, p)),
        out_shape=jax.ShapeDtypeStruct((bsz, lq, NA_HEADS * HEAD_DIM), BF16),
        scratch_shapes=[pltpu.VMEM((c + (NA_WIN_ROWS * GRID_W if has_lat else 0), 2 * LANES), F32)] * 2,
        compiler_params=_cparams(3),
        name="na_attention",
    )(*args)


def _rope_tables(seq):
    t = jnp.arange(seq, dtype=jnp.int32)
    row = (t // GRID_W).astype(F32)
    col = (t % GRID_W).astype(F32)
    n_freq = ROPE_DIM // 4
    inv = jnp.exp(-math.log(ROPE_BASE) * jnp.arange(n_freq, dtype=F32) / n_freq)
    ang = jnp.concatenate([row[:, None] * inv, col[:, None] * inv], axis=-1)
    cos, sin = jnp.cos(ang), jnp.sin(ang)
    c_head = jnp.concatenate([cos, cos], axis=-1)
    s_head = jnp.concatenate([-sin, sin], axis=-1)
    cs = jnp.stack([jnp.concatenate([c_head, c_head], axis=-1), jnp.concatenate([s_head, s_head], axis=-1)])
    cst = jnp.stack([c_head.T, s_head.T])
    return cs, cst


def _na_bias_table(rpb, n_rows):
    assert n_rows % 2 == 0 and n_rows >= NA_WIN_ROWS + 2
    rel_start = np.array([[0, 0], [0, 0], [0, 1], [2, 2], [2, 2]])
    dq = np.array([[0, 1], [2, 3], [4, 5], [6, 7], [8, 9]])
    jr = np.arange(NA_WIN_ROWS)
    row_ok = (jr >= rel_start[:, :, None]) & (jr < rel_start[:, :, None] + NA_KH)
    ridx = np.clip(jr - dq[:, :, None] + (NA_KH - 1), 0, 2 * NA_KH - 2)
    col = np.arange(GRID_W)
    col_start = np.clip(col - NA_KW // 2, 0, GRID_W - NA_KW)
    in_win = (col[None, :] >= col_start[:, None]) & (col[None, :] < col_start[:, None] + NA_KW)
    n_pairs = rpb.shape[0] // 2
    pad = GRID_W - NA_KW
    rows = jnp.pad((rpb.astype(F32)[:, ridx] * LOG2E)[..., ::-1], [(0, 0)] * 4 + [(pad, pad)])
    rows = rows.reshape(n_pairs, 2, NA_CLASSES, NA_STEP_ROWS, NA_WIN_ROWS, -1)
    t = jnp.stack([
        jnp.concatenate([rows[:, e, :, rr, :, GRID_W - 1 - kc:2 * GRID_W - 1 - kc]
                         for e in range(2) for rr in range(NA_STEP_ROWS)], axis=-1)
        for kc in range(GRID_W)], axis=-2)
    ok = row_ok.transpose(0, 2, 1)[:, :, None, :, None] & in_win.T[None, None, :, None, :]
    ok = np.broadcast_to(ok, (NA_CLASSES, NA_WIN_ROWS, GRID_W, NA_STEP_ROWS, GRID_W))
    ok = np.broadcast_to(ok[:, :, :, None], (NA_CLASSES, NA_WIN_ROWS, GRID_W, 2, NA_STEP_ROWS, GRID_W))
    t = jnp.where(ok.reshape(1, NA_CLASSES, NA_WIN_ROWS, GRID_W, 2 * LANES), t, NEG_INF)
    return t.reshape(n_pairs, NA_CLASSES, NA_WIN_ROWS * GRID_W, 2 * LANES)


def _swa_q_perm():
    idx = []
    for pair in range(SWA_KV_HEADS // 2):
        for g in range(SWA_GROUP):
            for e in range(2):
                head = (2 * pair + e) * SWA_GROUP + g
                idx.extend(range(head * HEAD_DIM, (head + 1) * HEAD_DIM))
    return np.asarray(idx)


def kernel(x, c, ctx, c_ctx, mod_w, mod_b, norm_g, final_norm_g, ffn_w_in, ffn_w_out,
           mla_w_down, mla_q_norm, mla_kv_norm, mla_w_uq, mla_w_ukv, mla_w_o,
           swa_w_qkv, swa_sink, swa_w_o, na_w_qkv, na_rpb, na_w_o,
           diff_w_qkv, diff_lambda, diff_norm_g, diff_w_o):
    bsz, seq, d = x.shape
    n_ctx = ctx.shape[1]
    depth = mod_w.shape[0]
    assert bsz + 1 <= MOD_ROWS and seq % GRID_W == 0 and n_ctx % LANES == 0

    cond = jnp.zeros((MOD_ROWS, d), F32).at[:bsz].set(c).at[bsz].set(c_ctx)
    mods = _mod_call(cond, mod_w, mod_b).reshape(depth, MOD_ROWS, N_MOD, d)
    cs, cst = _rope_tables(seq)
    xc = ctx.reshape(1, bsz * n_ctx, d)
    x_row = lambda b: b
    c_row = lambda b: bsz
    c64 = HEAD_DIM ** -0.5 * LOG2E

    def per_batch(t):
        return t.reshape(bsz, n_ctx, t.shape[-1])

    for layer in range(depth):
        kind = layer % 4
        j = layer // 4
        last = layer == depth - 1
        ml = mods[layer]
        w_in = ffn_w_in[layer].astype(BF16)
        w_out = ffn_w_out[layer].astype(BF16)

        x = _ffn(x, norm_g[layer, 0], ml, x_row, w_in[0], w_out[0], 0)
        xc = _ffn(xc, norm_g[layer, 0], ml, c_row, w_in[0], w_out[0], 0)
        g1 = norm_g[layer, 1]

        if kind == 0:
            h = MLA_HEADS
            qk = MLA_Q_RANK + MLA_KV_RANK
            c_mla = (MLA_NOPE + ROPE_DIM) ** -0.5 * LOG2E
            wd = mla_w_down[j].astype(BF16)
            z = jnp.zeros((d, ROPE_DIM), BF16)
            w1x = jnp.concatenate([wd, z], axis=1)
            w1c = jnp.concatenate([wd[:, :qk], z, wd[:, qk:]], axis=1)
            n1 = w1x.shape[1] // LANES
            p1x = _proj(x, d, 0, g1, ml, x_row, 3, w=w1x, kinds=[None] * (n1 - 1) + [True], cs=cs, out_dtype=F32)
            p1c = _proj(xc, d, 0, g1, ml, c_row, 3, w=w1c, kinds=[None] * n1, out_dtype=F32)
            wq = mla_w_uq[j].astype(BF16).T.reshape(h, MLA_NOPE + ROPE_DIM, MLA_Q_RANK)
            wqt = jnp.concatenate([wq, wq[:, MLA_NOPE:]], axis=1).reshape(-1, MLA_Q_RANK)
            wkv = mla_w_ukv[j].astype(BF16).reshape(MLA_KV_RANK, h, MLA_NOPE + MLA_V)
            wk = wkv[:, :, :MLA_NOPE].reshape(MLA_KV_RANK, -1)
            wvt = wkv[:, :, MLA_NOPE:].reshape(MLA_KV_RANK, -1).T
            kv_blk = MLA_Q_RANK // MLA_KV_RANK
            q_plain = [(False, c_mla)] * 4
            q_kinds = ([(False, c_mla)] * 2 + [(True, c_mla), (False, c_mla)]) * h
            v_kinds = [(False, 1.0)] * (2 * h)
            k_none = [None] * h
            qtx = _proj(p1x, MLA_Q_RANK, 0, mla_q_norm[j], None, None, 0, wt=wqt, kinds_t=q_kinds, cst=cst)
            kvx, vtx = _proj(p1x, MLA_KV_RANK, kv_blk, mla_kv_norm[j], None, None, 0, w=wk, kinds=k_none,
                             wt=wvt, kinds_t=v_kinds)
            kvc, vtc = _proj(p1c, MLA_KV_RANK, kv_blk, mla_kv_norm[j], None, None, 0, w=wk, kinds=k_none,
                             wt=wvt, kinds_t=v_kinds)
            kvc, p1c_b = per_batch(kvc), per_batch(p1c)
            ox = _mla_attn(qtx, kvc, p1c_b, vtc, kvx, p1x, vtx)
            w_o = mla_w_o[j].astype(BF16)
            if not last:
                qtc = _proj(p1c, MLA_Q_RANK, 0, mla_q_norm[j], None, None, 0, wt=wqt, kinds_t=q_plain * h)
                oc = _mla_attn(qtc, kvc, p1c_b, vtc)
        elif kind == 1:
            wqkv = swa_w_qkv[j].astype(BF16)
            nq = SWA_KV_HEADS * SWA_GROUP * HEAD_DIM
            nk = SWA_KV_HEADS * HEAD_DIM
            perm = _swa_q_perm()
            wqt = wqkv[:, :nq][:, perm].T
            wk, wvt = wqkv[:, nq:nq + nk], wqkv[:, nq + nk:].T
            gq, gv = nq // HEAD_DIM, nk // HEAD_DIM
            kx, ptx = _proj(x, d, 0, g1, ml, x_row, 3, w=wk, kinds=[True] * (nk // LANES), cs=cs,
                            wt=jnp.concatenate([wqt, wqt, wvt], axis=0),
                            kinds_t=[(True, c64)] * gq + [(False, c64)] * gq + [(False, 1.0)] * gv, cst=cst)
            kc, ptc = _proj(xc, d, 0, g1, ml, c_row, 3, w=wk, kinds=[None] * (nk // LANES),
                            wt=jnp.concatenate([wqt, wvt], axis=0),
                            kinds_t=[(False, c64)] * gq + [(False, 1.0)] * gv)
            kc = per_batch(kc)
            ox = _swa_attn(swa_sink[j], kc, ptc, kx, ptx)
            w_o = swa_w_o[j].astype(BF16)[perm, :]
            if not last:
                oc = _swa_attn(swa_sink[j], kc, ptc)
        elif kind == 2:
            wqkv = na_w_qkv[j].astype(BF16)
            nh = NA_HEADS * HEAD_DIM
            wk = wqkv[:, nh:2 * nh]
            wt = jnp.concatenate([wqkv[:, :nh], wqkv[:, 2 * nh:]], axis=1).T
            kt = [(False, c64)] * NA_HEADS + [(False, 1.0)] * NA_HEADS
            k_none = [None] * (nh // LANES)
            kx, ptx = _proj(x, d, 0, g1, ml, x_row, 3, w=wk, kinds=k_none, wt=wt, kinds_t=kt)
            kc, ptc = _proj(xc, d, 0, g1, ml, c_row, 3, w=wk, kinds=k_none, wt=wt, kinds_t=kt)
            kc = per_batch(kc)
            ox = _na_attn(kc, ptc, _na_bias_table(na_rpb[j], seq // GRID_W), kx, ptx)
            w_o = na_w_o[j].astype(BF16)
            if not last:
                oc = _na_attn(kc, ptc)
        else:
            h = DIFF_HEADS
            lam_init = 0.8 - 0.6 * math.exp(-0.3 * layer)
            wqkv = diff_w_qkv[j].astype(BF16)
            nq = 2 * h * HEAD_DIM
            wk = wqkv[:, nq:2 * nq]
            wqt = wqkv[:, :nq].T
            wvt = wqkv[:, 2 * nq:].T
            wq2 = wqt.reshape(h, 2 * HEAD_DIM, d)
            wtx = jnp.concatenate([jnp.concatenate([wq2, wq2], axis=1).reshape(-1, d), wvt], axis=0)
            ktx = ([(True, c64)] * 2 + [(False, c64)] * 2) * h + [(False, 1.0)] * (2 * h)
            ktc = [(False, c64)] * (2 * h) + [(False, 1.0)] * (2 * h)
            kx, ptx = _proj(x, d, 0, g1, ml, x_row, 3, w=wk, kinds=[True] * h, cs=cs, wt=wtx, kinds_t=ktx, cst=cst)
            kc, ptc = _proj(xc, d, 0, g1, ml, c_row, 3, w=wk, kinds=[None] * h,
                            wt=jnp.concatenate([wqt, wvt], axis=0), kinds_t=ktc)
            kc = per_batch(kc)
            ox = _diff_attn(kc, ptc, diff_lambda[j], diff_norm_g[j], lam_init, 0, h, kx, ptx)
            w_o = diff_w_o[j].astype(BF16)
            if not last:
                oc = _diff_attn(kc, ptc, diff_lambda[j], diff_norm_g[j], lam_init, 0, h)

        x = _oproj(x, ox, w_o, ml, x_row, 5)
        x = _ffn(x, norm_g[layer, 2], ml, x_row, w_in[1], w_out[1], 6,
                 final_g=final_norm_g if last else None)
        if not last:
            xc = _oproj(xc, oc.reshape(1, bsz * n_ctx, oc.shape[-1]), w_o, ml, c_row, 5)
            xc = _ffn(xc, norm_g[layer, 2], ml, c_row, w_in[1], w_out[1], 6)
    return x
```

```python
import functools
import math

import jax
import jax.numpy as jnp
import numpy as np
from jax import lax
from jax.experimental import pallas as pl
from jax.experimental.pallas import tpu as pltpu

F32 = jnp.float32
BF16 = jnp.bfloat16

GRID_W = 64
N_MOD = 9
EPS = 1e-6
ROPE_BASE = 10000.0
ROPE_DIM = 64
NEG_INF = -1e30
LOG2E = math.log2(math.e)

MLA_HEADS = 8
MLA_Q_RANK = 512
MLA_KV_RANK = 256
MLA_NOPE = 128
MLA_V = 128

SWA_KV_HEADS = 4
SWA_GROUP = 4
SWA_WINDOW = 128

NA_HEADS = 16
NA_KH = 8
NA_KW = 16
NA_STEP_ROWS = 2
NA_WIN_ROWS = NA_KH + NA_STEP_ROWS
NA_CLASSES = 5

DIFF_HEADS = 8
HEAD_DIM = 64

LANES = 128
MXU_WIDTH = 256
VMEM_LIMIT_BYTES = 56 * 1024 * 1024
MOD_ROWS = 8


def _cparams(n_axes):
    return pltpu.CompilerParams(
        dimension_semantics=("arbitrary",) * n_axes,
        vmem_limit_bytes=VMEM_LIMIT_BYTES,
    )


def _resident(shape, index_map):
    return pl.BlockSpec(shape, index_map, pipeline_mode=pl.Buffered(1))


def _rms(x):
    return x * lax.rsqrt(jnp.mean(x * x, axis=-1, keepdims=True) + EPS)


def _norm_mod(x, g, mod_ref, base):
    y = _rms(x) * g
    if mod_ref is not None:
        y = y * (1.0 + mod_ref[base + 1:base + 2, :]) + mod_ref[base:base + 1, :]
    return y


def _dot(a, b):
    return jnp.dot(a, b, preferred_element_type=F32)


def _dot_nt(a, b):
    return lax.dot_general(a, b, (((1,), (1,)), ((), ())), preferred_element_type=F32)


def _row_is_low(shape):
    return lax.broadcasted_iota(jnp.int32, shape, 0) < HEAD_DIM


def _stack_heads_t(qt):
    low = _row_is_low(qt.shape)
    zero = jnp.zeros_like(qt)
    return jnp.concatenate([jnp.where(low, qt, zero), jnp.where(low, zero, qt)], axis=1)


def _rope_tile(y, c, s):
    lane = lax.broadcasted_iota(jnp.int32, y.shape, 1)
    first = (lane & (HEAD_DIM - 1)) < HEAD_DIM // 2
    swapped = jnp.where(first, pltpu.roll(y, LANES - HEAD_DIM // 2, 1), pltpu.roll(y, HEAD_DIM // 2, 1))
    return y * c + swapped * s


class _SoftmaxT:
    def __init__(self, m, l, acc):
        self.m, self.l, self.acc = m, l, acc

    @staticmethod
    def first(s, vt):
        m = jnp.max(s, axis=0, keepdims=True)
        p = jnp.exp2(s - m)
        return _SoftmaxT(m, jnp.sum(p, axis=0, keepdims=True), _dot(vt, p.astype(vt.dtype)))

    def update(self, s, vt):
        m_new = jnp.maximum(self.m, jnp.max(s, axis=0, keepdims=True))
        a = jnp.exp2(self.m - m_new)
        p = jnp.exp2(s - m_new)
        return _SoftmaxT(m_new, a * self.l + jnp.sum(p, axis=0, keepdims=True),
                         a * self.acc + _dot(vt, p.astype(vt.dtype)))


def _attend(qt_ctx, kc, vtc, bufs, qt_lat=None, k_ref=None, vt_ref=None, tk=0):
    c = kc.shape[0]
    n = 1 + (k_ref.shape[0] // tk if k_ref is not None else 0)

    def write_scores(j):
        if j == 0:
            bufs[0][0:c, :] = _dot(kc, qt_ctx)
        else:
            bufs[j % 2][0:tk, :] = _dot(k_ref[(j - 1) * tk:j * tk, :], qt_lat)

    write_scores(0)
    st = None
    for j in range(n):
        if j + 1 < n:
            write_scores(j + 1)
        if j == 0:
            st = _SoftmaxT.first(bufs[0][0:c, :], vtc)
        else:
            st = st.update(bufs[j % 2][0:tk, :], vt_ref[:, (j - 1) * tk:j * tk])
    return st


def _mod_body(c_ref, w_ref, b_ref, o_ref):
    c = c_ref[...]
    h = (c * jax.nn.sigmoid(c)).astype(BF16)
    o_ref[...] = _dot(h, w_ref[...].astype(BF16)) + b_ref[...]


def _mod_call(cond, mod_w, mod_b):
    depth, d, n = mod_w.shape
    tn = 1024
    return pl.pallas_call(
        _mod_body,
        grid=(depth, n // tn),
        in_specs=[
            pl.BlockSpec((MOD_ROWS, d), lambda l, j: (0, 0)),
            pl.BlockSpec((None, d, tn), lambda l, j: (l, 0, j)),
            pl.BlockSpec((None, 1, tn), lambda l, j: (l, 0, j)),
        ],
        out_specs=pl.BlockSpec((None, MOD_ROWS, tn), lambda l, j: (l, 0, j)),
        out_shape=jax.ShapeDtypeStruct((depth, MOD_ROWS, n), F32),
        compiler_params=_cparams(2),
        name="adaln_mod",
    )(cond, mod_w, mod_b.reshape(depth, 1, n))


def _ffn_body(x_ref, g_ref, mod_ref, win_ref, wout_ref, *rest, base, ff, chunk, final, mixer_base):
    rest = list(rest)
    a_ref, wo_ref = (rest.pop(0), rest.pop(0)) if mixer_base is not None else (None, None)
    fg_ref = rest.pop(0) if final else None
    o_ref, act_ref = rest
    x = x_ref[...]
    if mixer_base is not None:
        x = x + mod_ref[mixer_base:mixer_base + 1, :] * _dot(a_ref[...], wo_ref[...])
    hb = _norm_mod(x, g_ref[...], mod_ref, base).astype(BF16)
    for c0 in range(0, ff, chunk):
        g = _dot(hb, win_ref[:, c0:c0 + chunk])
        u = _dot(hb, win_ref[:, ff + c0:ff + c0 + chunk])
        act_ref[:, c0:c0 + chunk] = (g * jax.nn.sigmoid(g) * u).astype(BF16)
    y = x + (0.5 * mod_ref[base + 2:base + 3, :]) * _dot(act_ref[...], wout_ref[...])
    if final:
        y = _rms(y) * fg_ref[...]
    o_ref[...] = y


def _ffn(x, g, mods_l, mod_row, w_in, w_out, layer, half, base, final_g=None, mixer=None):
    bx, n, d = x.shape
    ff = w_out.shape[2]
    tm = min(512, n)
    final = final_g is not None
    in_specs = [
        pl.BlockSpec((None, tm, d), lambda b, i: (b, i, 0)),
        pl.BlockSpec((1, d), lambda b, i: (0, 0)),
        pl.BlockSpec((None, N_MOD, d), lambda b, i: (mod_row(b), 0, 0)),
        _resident((None, None, d, 2 * ff), lambda b, i: (layer, half, 0, 0)),
        _resident((None, None, ff, d), lambda b, i: (layer, half, 0, 0)),
    ]
    args = [x, g.reshape(1, d), mods_l, w_in, w_out]
    mixer_base = None
    if mixer is not None:
        a, w_o, mixer_base = mixer
        k = a.shape[-1]
        in_specs += [pl.BlockSpec((None, tm, k), lambda b, i: (b, i, 0)), _resident((k, d), lambda b, i: (0, 0))]
        args += [a, w_o]
    if final:
        in_specs.append(pl.BlockSpec((1, d), lambda b, i: (0, 0)))
        args.append(final_g.reshape(1, d))
    return pl.pallas_call(
        functools.partial(_ffn_body, base=base, ff=ff, chunk=MXU_WIDTH, final=final, mixer_base=mixer_base),
        grid=(bx, n // tm),
        in_specs=in_specs,
        out_specs=pl.BlockSpec((None, tm, d), lambda b, i: (b, i, 0)),
        out_shape=jax.ShapeDtypeStruct(x.shape, F32),
        scratch_shapes=[pltpu.VMEM((tm, ff), BF16)],
        compiler_params=_cparams(2),
        name="ffn_swiglu",
    )(*args)


PROJ_T_ROWS = 512


def _proj_body(*refs, base, use_mod, kinds, kinds_t):
    it = iter(refs)
    x_ref, g_ref = next(it), next(it)
    mod_ref = next(it) if use_mod else None
    w_ref = next(it) if kinds else None
    cs_ref = next(it) if any(k is not None for k in kinds) else None
    wt_ref = next(it) if kinds_t else None
    cst_ref = next(it) if _plan_ropes(kinds_t) else None
    o_ref = next(it) if kinds else None
    ot_ref = next(it) if kinds_t else None
    hb = _norm_mod(x_ref[...].astype(F32), g_ref[...], mod_ref, base).astype(BF16)

    n_tiles = len(kinds)
    t = 0
    while t < n_tiles:
        w = min(MXU_WIDTH // LANES, n_tiles - t)
        y = _dot(hb, w_ref[:, t * LANES:(t + w) * LANES])
        for j in range(w):
            yj = y[:, j * LANES:(j + 1) * LANES]
            if kinds[t + j] is not None:
                yj = _rope_tile(yj, cs_ref[0], cs_ref[1])
            o_ref[:, (t + j) * LANES:(t + j + 1) * LANES] = yj.astype(o_ref.dtype)
        t += w

    half = HEAD_DIM // 2
    n_rows_t = len(kinds_t) * HEAD_DIM
    for r0 in range(0, n_rows_t, PROJ_T_ROWS):
        rows = min(PROJ_T_ROWS, n_rows_t - r0)
        yt = _dot_nt(wt_ref[r0:r0 + rows, :], hb)
        for gi in range(rows // HEAD_DIM):
            src = yt[gi * HEAD_DIM:(gi + 1) * HEAD_DIM]
            for dst, rope, scale in kinds_t[r0 // HEAD_DIM + gi]:
                tile = src
                if rope:
                    swapped = jnp.concatenate([src[half:], src[:half]], axis=0)
                    tile = src * cst_ref[0] + swapped * cst_ref[1]
                if scale != 1.0:
                    tile = tile * scale
                ot_ref[dst * HEAD_DIM:(dst + 1) * HEAD_DIM, :] = tile.astype(ot_ref.dtype)


def _plan_ropes(plan):
    return any(rope for outs in plan for _, rope, _ in outs)


def _plan(groups):
    return [((g, rope, scale),) for g, (rope, scale) in enumerate(groups)]


def _proj(x, in_width, in_block, g, mods_l, mod_row, base, w=None, kinds=(), cs=None, out_dtype=BF16,
          wt=None, kinds_t=(), cst=None):
    bx, n, _ = x.shape
    tm = min(512, n)
    use_mod = mods_l is not None
    kinds, kinds_t = tuple(kinds), tuple(kinds_t)
    in_specs = [
        pl.BlockSpec((None, tm, in_width), lambda b, i: (b, i, in_block)),
        pl.BlockSpec((1, in_width), lambda b, i: (0, 0)),
    ]
    args = [x, g.reshape(1, in_width)]
    out_specs, out_shape = [], []
    if use_mod:
        in_specs.append(pl.BlockSpec((None, N_MOD, in_width), lambda b, i: (mod_row(b), 0, 0)))
        args.append(mods_l)
    if kinds:
        f = w.shape[1]
        assert f == len(kinds) * LANES
        in_specs.append(_resident((in_width, f), lambda b, i: (0, 0)))
        args.append(w)
        if any(k is not None for k in kinds):
            in_specs.append(pl.BlockSpec((2, tm, LANES), lambda b, i: (0, i, 0)))
            args.append(cs)
        out_specs.append(pl.BlockSpec((None, tm, f), lambda b, i: (b, i, 0)))
        out_shape.append(jax.ShapeDtypeStruct((bx, n, f), out_dtype))
    if kinds_t:
        fw = wt.shape[0]
        assert fw == len(kinds_t) * HEAD_DIM
        dsts = sorted(dst for outs in kinds_t for dst, _, _ in outs)
        assert dsts == list(range(len(dsts)))
        ft = len(dsts) * HEAD_DIM
        in_specs.append(_resident((fw, in_width), lambda b, i: (0, 0)))
        args.append(wt)
        if _plan_ropes(kinds_t):
            in_specs.append(pl.BlockSpec((2, HEAD_DIM, tm), lambda b, i: (0, 0, i)))
            args.append(cst)
        out_specs.append(pl.BlockSpec((None, ft, tm), lambda b, i: (b, 0, i)))
        out_shape.append(jax.ShapeDtypeStruct((bx, ft, n), BF16))
    outs = pl.pallas_call(
        functools.partial(_proj_body, base=base, use_mod=use_mod, kinds=kinds, kinds_t=kinds_t),
        grid=(bx, n // tm),
        in_specs=in_specs,
        out_specs=out_specs,
        out_shape=out_shape,
        compiler_params=_cparams(2),
        name="norm_proj",
    )(*args)
    return outs[0] if len(outs) == 1 else outs


def _lat_rows(blk):
    return lambda b, h, i: (b, 0, blk(h))


def _lat_t(blk):
    return lambda b, h, i: (b, blk(h), 0)


def _ctx_t(blk):
    return lambda b, h, i: (0, blk(h), b)


def _mla_body(*refs, tk, has_lat):
    if has_lat:
        qt_ref, kn_ref, kr_ref, vt_ref, knc_ref, krc_ref, vtc_ref, o_ref, s_a, s_b, kcat_ref = refs
    else:
        qt_ref, knc_ref, krc_ref, vtc_ref, o_ref, s_a, s_b = refs
        kcat_ref = vt_ref = None
    qt = qt_ref[...]
    kc = jnp.concatenate([knc_ref[...], krc_ref[...].astype(BF16)], axis=1)
    if has_lat:
        @pl.when(pl.program_id(2) == 0)
        def _():
            kcat_ref[:, :LANES] = kn_ref[...]
            kcat_ref[:, LANES:] = kr_ref[...].astype(BF16)

    st = _attend(qt, kc, vtc_ref[...], (s_a, s_b), qt, kcat_ref, vt_ref, tk)
    o_ref[...] = (st.acc / st.l).T.astype(o_ref.dtype)


def _mla_attn(qt, kvc, p1c, vtc, kv=None, p1=None, vt=None):
    h = MLA_HEADS
    has_lat = kv is not None
    bsz, c = kvc.shape[0], kvc.shape[1]
    lq = qt.shape[2] if has_lat else c
    tq = min(1024, lq)
    rope_tile = (MLA_Q_RANK + MLA_KV_RANK) // LANES
    head = lambda hh: hh
    in_specs, args, scratch, tk = [], [qt], [], 0
    if has_lat:
        lk = kv.shape[1]
        tk = min(512, lk)
        in_specs += [
            pl.BlockSpec((None, 2 * LANES, tq), lambda b, hh, i: (b, hh, i)),
            pl.BlockSpec((None, lk, LANES), _lat_rows(head)),
            pl.BlockSpec((None, lk, LANES), _lat_rows(lambda hh: rope_tile)),
            pl.BlockSpec((None, LANES, lk), _lat_t(head)),
        ]
        args += [kv, p1, vt]
        scratch = [pltpu.VMEM((lk, 2 * LANES), BF16)]
    else:
        in_specs.append(pl.BlockSpec((None, 2 * LANES, tq), _ctx_t(head)))
    in_specs += [
        pl.BlockSpec((None, c, LANES), _lat_rows(head)),
        pl.BlockSpec((None, c, LANES), _lat_rows(lambda hh: rope_tile)),
        pl.BlockSpec((None, LANES, c), _ctx_t(head)),
    ]
    args += [kvc, p1c, vtc]
    return pl.pallas_call(
        functools.partial(_mla_body, tk=tk, has_lat=has_lat),
        grid=(bsz, h, lq // tq),
        in_specs=in_specs,
        out_specs=pl.BlockSpec((None, tq, LANES), lambda b, hh, i: (b, i, hh)),
        out_shape=jax.ShapeDtypeStruct((bsz, lq, h * MLA_V), BF16),
        scratch_shapes=[pltpu.VMEM((max(tk, c), tq), F32)] * 2 + scratch,
        compiler_params=_cparams(3),
        name="mla_attention",
    )(*args)


def _diff_body(*refs, lam_init, tk, has_lat):
    if has_lat:
        qt_ref, k_ref, vt_ref, kc_ref, vtc_ref, lam_ref, ng_ref, o_ref, s_a, s_b = refs
        q_plain, q_rot = _stack_heads_t(qt_ref[LANES:, :]), _stack_heads_t(qt_ref[:LANES, :])
    else:
        qt_ref, kc_ref, vtc_ref, lam_ref, ng_ref, o_ref, s_a, s_b = refs
        q_plain, q_rot, k_ref, vt_ref = _stack_heads_t(qt_ref[...]), None, None, None
    tq = qt_ref.shape[1]
    st = _attend(q_plain, kc_ref[...], vtc_ref[...], (s_a, s_b), q_rot, k_ref, vt_ref, tk)
    lp = lam_ref[...]
    lam = (jnp.exp(jnp.sum(lp[0:1] * lp[1:2], axis=-1, keepdims=True))
           - jnp.exp(jnp.sum(lp[2:3] * lp[3:4], axis=-1, keepdims=True)) + lam_init)
    o = st.acc / st.l
    o = (o[:, :tq] - lam * o[:, tq:]).T
    o_ref[...] = (_rms(o) * ng_ref[...] * (1.0 - lam_init)).astype(o_ref.dtype)


def _diff_attn(kc, ptc, lam_params, norm_g, lam_init, qtc_blk, vtc_blk, kx=None, ptx=None):
    h = DIFF_HEADS
    has_lat = kx is not None
    bsz, c = kc.shape[0], kc.shape[1]
    lq = kx.shape[1] if has_lat else c
    tq = min(512, lq)
    head = lambda hh: hh
    in_specs, args, tk = [], [], 0
    if has_lat:
        lk = kx.shape[1]
        tk = min(512, lk)
        in_specs += [
            pl.BlockSpec((None, 2 * LANES, tq), lambda b, hh, i: (b, hh, i)),
            pl.BlockSpec((None, lk, LANES), _lat_rows(head)),
            pl.BlockSpec((None, LANES, lk), _lat_t(lambda hh: 2 * h + hh)),
        ]
        args += [ptx, kx, ptx]
    else:
        in_specs.append(pl.BlockSpec((None, LANES, tq), _ctx_t(lambda hh: qtc_blk + hh)))
        args.append(ptc)
    in_specs += [
        pl.BlockSpec((None, c, LANES), _lat_rows(head)),
        pl.BlockSpec((None, LANES, c), _ctx_t(lambda hh: vtc_blk + hh)),
        pl.BlockSpec((4, HEAD_DIM), lambda b, hh, i: (0, 0)),
        pl.BlockSpec((1, 2 * HEAD_DIM), lambda b, hh, i: (0, 0)),
    ]
    args += [kc, ptc, lam_params, norm_g.reshape(1, 2 * HEAD_DIM)]
    return pl.pallas_call(
        functools.partial(_diff_body, lam_init=lam_init, tk=tk, has_lat=has_lat),
        grid=(bsz, h, lq // tq),
        in_specs=in_specs,
        out_specs=pl.BlockSpec((None, tq, LANES), lambda b, hh, i: (b, i, hh)),
        out_shape=jax.ShapeDtypeStruct((bsz, lq, h * 2 * HEAD_DIM), BF16),
        scratch_shapes=[pltpu.VMEM((max(tk, c), 2 * tq), F32)] * 2,
        compiler_params=_cparams(3),
        name="diff_attention",
    )(*args)


def _swa_body(*refs, tq, has_lat):
    if has_lat:
        sink_ref, qr_ref, qp_ref, k_ref, vt_ref, kc_ref, vtc_ref, o_ref, s_a, s_b = refs
    else:
        sink_ref, qp_ref, kc_ref, vtc_ref, o_ref, s_a, s_b = refs
    bufs = (s_a, s_b)
    pair = pl.program_id(1)
    kc, vtc = kc_ref[...], vtc_ref[...]
    c = kc.shape[0]
    span = tq + 2 * SWA_WINDOW if has_lat else 0
    first_half = lax.broadcasted_iota(jnp.int32, (1, 2 * tq), 1) < tq
    if has_lat:
        lk = k_ref.shape[0]
        q0 = pl.program_id(2) * tq
        start = pl.multiple_of(jnp.clip(q0 - SWA_WINDOW, 0, lk - span), SWA_WINDOW)
        keys = pl.ds(start, span)
        kpos = start + lax.broadcasted_iota(jnp.int32, (span, 2 * tq), 0)
        qpos = q0 + (lax.broadcasted_iota(jnp.int32, (span, 2 * tq), 1) & (tq - 1))
        valid = jnp.abs(kpos - qpos) <= SWA_WINDOW
    low = _row_is_low((LANES, tq))

    def scores(g):
        rows = slice(g * LANES, (g + 1) * LANES)
        buf = bufs[g % 2]
        buf[0:c, :] = _dot(kc, _stack_heads_t(qp_ref[rows, :]))
        if has_lat:
            s_w = _dot(k_ref[keys, :], _stack_heads_t(qr_ref[rows, :]))
            buf[c:c + span, :] = jnp.where(valid, s_w, NEG_INF)

    scores(0)
    for g in range(SWA_GROUP):
        if g + 1 < SWA_GROUP:
            scores(g + 1)
        sink = jnp.where(first_half, sink_ref[2 * pair * SWA_GROUP + g],
                         sink_ref[(2 * pair + 1) * SWA_GROUP + g]) * LOG2E
        s = bufs[g % 2][...]
        m = jnp.maximum(jnp.max(s, axis=0, keepdims=True), sink)
        p = jnp.exp2(s - m)
        l = jnp.sum(p, axis=0, keepdims=True) + jnp.exp2(sink - m)
        p = p.astype(BF16)
        o = _dot(vtc, p[0:c])
        if has_lat:
            o = o + _dot(vt_ref[:, keys], p[c:])
        o = o / l
        o_ref[:, g * LANES:(g + 1) * LANES] = jnp.where(low, o[:, :tq], o[:, tq:]).T.astype(o_ref.dtype)


def _swa_attn(sink, kc, ptc, kx=None, ptx=None):
    n_pairs = SWA_KV_HEADS // 2
    qw = SWA_GROUP * LANES
    has_lat = kx is not None
    bsz, c = kc.shape[0], kc.shape[1]
    lq = kx.shape[1] if has_lat else c
    tq = min(256, lq)
    pair = lambda p: p
    in_specs = [pl.BlockSpec(memory_space=pltpu.SMEM)]
    args = [sink]
    span = 0
    if has_lat:
        lk = kx.shape[1]
        span = tq + 2 * SWA_WINDOW
        in_specs += [
            pl.BlockSpec((None, qw, tq), lambda b, p, i: (b, p, i)),
            pl.BlockSpec((None, qw, tq), lambda b, p, i: (b, n_pairs + p, i)),
            pl.BlockSpec((None, lk, LANES), _lat_rows(pair)),
            pl.BlockSpec((None, LANES, lk), _lat_t(lambda p: 2 * n_pairs * SWA_GROUP + p)),
        ]
        args += [ptx, ptx, kx, ptx]
    else:
        in_specs.append(pl.BlockSpec((None, qw, tq), _ctx_t(pair)))
        args.append(ptc)
    in_specs += [
        pl.BlockSpec((None, c, LANES), _lat_rows(pair)),
        pl.BlockSpec((None, LANES, c), _ctx_t(lambda p: n_pairs * SWA_GROUP + p)),
    ]
    args += [kc, ptc]
    return pl.pallas_call(
        functools.partial(_swa_body, tq=tq, has_lat=has_lat),
        grid=(bsz, n_pairs, lq // tq),
        in_specs=in_specs,
        out_specs=pl.BlockSpec((None, tq, qw), lambda b, p, i: (b, i, p)),
        out_shape=jax.ShapeDtypeStruct((bsz, lq, n_pairs * qw), BF16),
        scratch_shapes=[pltpu.VMEM((c + span, 2 * tq), F32)] * 2,
        compiler_params=_cparams(3),
        name="swa_attention",
    )(*args)


def _na_body(*refs, n_rows, has_lat):
    if has_lat:
        qt_ref, k_ref, vt_ref, bias_ref, kc_ref, vtc_ref, o_ref, s_a, s_b = refs
    else:
        qt_ref, kc_ref, vtc_ref, o_ref, s_a, s_b = refs
    bufs = (s_a, s_b)
    kc, vtc = kc_ref[...], vtc_ref[...]
    c = kc.shape[0]
    n_steps = qt_ref.shape[1] // LANES
    win = NA_WIN_ROWS * GRID_W
    low = _row_is_low((LANES, LANES))

    def scores(u):
        q = _stack_heads_t(qt_ref[:, u * LANES:(u + 1) * LANES])
        buf = bufs[u % 2]
        buf[0:c, :] = _dot(kc, q)
        if not has_lat:
            return None
        r = (pl.program_id(2) * n_steps + u) * NA_STEP_ROWS
        ws = jnp.clip(r - NA_KH // 2, 0, n_rows - NA_WIN_ROWS)
        cls = lax.shift_right_logical(r - ws, 1)
        keys = pl.ds(pl.multiple_of(ws * GRID_W, LANES), win)
        buf[c:c + win, :] = _dot(k_ref[keys, :], q) + bias_ref[cls]
        return keys

    keys = scores(0)
    for u in range(n_steps):
        next_keys = scores(u + 1) if u + 1 < n_steps else None
        s = bufs[u % 2][...]
        p = jnp.exp2(s - jnp.max(s, axis=0, keepdims=True))
        l = jnp.sum(p, axis=0, keepdims=True)
        p = p.astype(BF16)
        o = _dot(vtc, p[0:c])
        if has_lat:
            o = o + _dot(vt_ref[:, keys], p[c:])
        o = o / l
        o_ref[u * LANES:(u + 1) * LANES, :] = jnp.where(low, o[:, :LANES], o[:, LANES:]).T.astype(o_ref.dtype)
        keys = next_keys


def _na_attn(kc, ptc, bias=None, kx=None, ptx=None):
    n_pairs = NA_HEADS // 2
    has_lat = kx is not None
    bsz, c = kc.shape[0], kc.shape[1]
    pair = lambda p: p
    in_specs, args, n_rows = [], [], 0
    if has_lat:
        lq = kx.shape[1]
        n_rows = lq // GRID_W
        tq = min(512, lq)
        in_specs += [
            pl.BlockSpec((None, LANES, tq), lambda b, p, i: (b, p, i)),
            pl.BlockSpec((None, lq, LANES), _lat_rows(pair)),
            pl.BlockSpec((None, LANES, lq), _lat_t(lambda p: n_pairs + p)),
            pl.BlockSpec((None, NA_CLASSES, NA_WIN_ROWS * GRID_W, 2 * LANES), lambda b, p, i: (p, 0, 0, 0)),
        ]
        args += [ptx, kx, ptx, bias]
    else:
        lq = tq = c
        in_specs.append(pl.BlockSpec((None, LANES, tq), _ctx_t(pair)))
        args.append(ptc)
    in_specs += [
        pl.BlockSpec((None, c, LANES), _lat_rows(pair)),
        pl.BlockSpec((None, LANES, c), _ctx_t(lambda p: n_pairs + p)),
    ]
    args += [kc, ptc]
    return pl.pallas_call(
        functools.partial(_na_body, n_rows=n_rows, has_lat=has_lat),
        grid=(bsz, n_pairs, lq // tq),
        in_specs=in_specs,
        out_specs=pl.BlockSpec((None, tq, LANES), lambda b, p, i: (b, i, p)),
        out_shape=jax.ShapeDtypeStruct((bsz, lq, NA_HEADS * HEAD_DIM), BF16),
        scratch_shapes=[pltpu.VMEM((c + (NA_WIN_ROWS * GRID_W if has_lat else 0), 2 * LANES), F32)] * 2,
        compiler_params=_cparams(3),
        name="na_attention",
    )(*args)


def _rope_tables(seq):
    t = jnp.arange(seq, dtype=jnp.int32)
    row = (t // GRID_W).astype(F32)
    col = (t % GRID_W).astype(F32)
    n_freq = ROPE_DIM // 4
    inv = jnp.exp(-math.log(ROPE_BASE) * jnp.arange(n_freq, dtype=F32) / n_freq)
    ang = jnp.concatenate([row[:, None] * inv, col[:, None] * inv], axis=-1)
    cos, sin = jnp.cos(ang), jnp.sin(ang)
    c_head = jnp.concatenate([cos, cos], axis=-1)
    s_head = jnp.concatenate([-sin, sin], axis=-1)
    cs = jnp.stack([jnp.concatenate([c_head, c_head], axis=-1), jnp.concatenate([s_head, s_head], axis=-1)])
    cst = jnp.stack([c_head.T, s_head.T])
    return cs, cst


def _na_bias_table(rpb, n_rows):
    assert n_rows % 2 == 0 and n_rows >= NA_WIN_ROWS + 2
    rel_start = np.array([[0, 0], [0, 0], [0, 1], [2, 2], [2, 2]])
    dq = np.array([[0, 1], [2, 3], [4, 5], [6, 7], [8, 9]])
    jr = np.arange(NA_WIN_ROWS)
    row_ok = (jr >= rel_start[:, :, None]) & (jr < rel_start[:, :, None] + NA_KH)
    ridx = np.clip(jr - dq[:, :, None] + (NA_KH - 1), 0, 2 * NA_KH - 2)
    col = np.arange(GRID_W)
    col_start = np.clip(col - NA_KW // 2, 0, GRID_W - NA_KW)
    in_win = (col[None, :] >= col_start[:, None]) & (col[None, :] < col_start[:, None] + NA_KW)
    n_pairs = rpb.shape[0] // 2
    pad = GRID_W - NA_KW
    rows = jnp.pad((rpb.astype(F32)[:, ridx] * LOG2E)[..., ::-1], [(0, 0)] * 4 + [(pad, pad)])
    rows = rows.reshape(n_pairs, 2, NA_CLASSES, NA_STEP_ROWS, NA_WIN_ROWS, -1)
    t = jnp.stack([
        jnp.concatenate([rows[:, e, :, rr, :, GRID_W - 1 - kc:2 * GRID_W - 1 - kc]
                         for e in range(2) for rr in range(NA_STEP_ROWS)], axis=-1)
        for kc in range(GRID_W)], axis=-2)
    ok = row_ok.transpose(0, 2, 1)[:, :, None, :, None] & in_win.T[None, None, :, None, :]
    ok = np.broadcast_to(ok, (NA_CLASSES, NA_WIN_ROWS, GRID_W, NA_STEP_ROWS, GRID_W))
    ok = np.broadcast_to(ok[:, :, :, None], (NA_CLASSES, NA_WIN_ROWS, GRID_W, 2, NA_STEP_ROWS, GRID_W))
    t = jnp.where(ok.reshape(1, NA_CLASSES, NA_WIN_ROWS, GRID_W, 2 * LANES), t, NEG_INF)
    return t.reshape(n_pairs, NA_CLASSES, NA_WIN_ROWS * GRID_W, 2 * LANES)


def _swa_q_perm():
    idx = []
    for pair in range(SWA_KV_HEADS // 2):
        for g in range(SWA_GROUP):
            for e in range(2):
                head = (2 * pair + e) * SWA_GROUP + g
                idx.extend(range(head * HEAD_DIM, (head + 1) * HEAD_DIM))
    return np.asarray(idx)


def kernel(x, c, ctx, c_ctx, mod_w, mod_b, norm_g, final_norm_g, ffn_w_in, ffn_w_out,
           mla_w_down, mla_q_norm, mla_kv_norm, mla_w_uq, mla_w_ukv, mla_w_o,
           swa_w_qkv, swa_sink, swa_w_o, na_w_qkv, na_rpb, na_w_o,
           diff_w_qkv, diff_lambda, diff_norm_g, diff_w_o):
    bsz, seq, d = x.shape
    n_ctx = ctx.shape[1]
    depth = mod_w.shape[0]
    assert bsz + 1 <= MOD_ROWS and seq % GRID_W == 0 and n_ctx % LANES == 0

    cond = jnp.zeros((MOD_ROWS, d), F32).at[:bsz].set(c).at[bsz].set(c_ctx)
    mods = _mod_call(cond, mod_w, mod_b).reshape(depth, MOD_ROWS, N_MOD, d)
    cs, cst = _rope_tables(seq)
    w_in, w_out = ffn_w_in.astype(BF16), ffn_w_out.astype(BF16)
    xc = ctx.reshape(1, bsz * n_ctx, d)
    x_row = lambda b: b
    c_row = lambda b: bsz
    c64 = HEAD_DIM ** -0.5 * LOG2E

    def per_batch(t):
        return t.reshape(bsz, n_ctx, t.shape[-1])

    for layer in range(depth):
        kind = layer % 4
        j = layer // 4
        last = layer == depth - 1
        ml = mods[layer]

        x = _ffn(x, norm_g[layer, 0], ml, x_row, w_in, w_out, layer, 0, 0)
        xc = _ffn(xc, norm_g[layer, 0], ml, c_row, w_in, w_out, layer, 0, 0)
        g1 = norm_g[layer, 1]

        if kind == 0:
            h = MLA_HEADS
            qk = MLA_Q_RANK + MLA_KV_RANK
            c_mla = (MLA_NOPE + ROPE_DIM) ** -0.5 * LOG2E
            wd = mla_w_down[j].astype(BF16)
            z = jnp.zeros((d, ROPE_DIM), BF16)
            w1x = jnp.concatenate([wd, z], axis=1)
            w1c = jnp.concatenate([wd[:, :qk], z, wd[:, qk:]], axis=1)
            n1 = w1x.shape[1] // LANES
            p1x = _proj(x, d, 0, g1, ml, x_row, 3, w=w1x, kinds=[None] * (n1 - 1) + [True], cs=cs, out_dtype=F32)
            p1c = _proj(xc, d, 0, g1, ml, c_row, 3, w=w1c, kinds=[None] * n1, out_dtype=F32)
            wqt = mla_w_uq[j].astype(BF16).T
            wkv = mla_w_ukv[j].astype(BF16).reshape(MLA_KV_RANK, h, MLA_NOPE + MLA_V)
            wk = wkv[:, :, :MLA_NOPE].reshape(MLA_KV_RANK, -1)
            wvt = wkv[:, :, MLA_NOPE:].reshape(MLA_KV_RANK, -1).T
            kv_blk = MLA_Q_RANK // MLA_KV_RANK
            q_kinds, q_plain = [], []
            for hh in range(h):
                o4 = 4 * hh
                nope = [((o4, False, c_mla),), ((o4 + 1, False, c_mla),)]
                q_kinds += nope + [((o4 + 2, True, c_mla), (o4 + 3, False, c_mla))]
                q_plain += nope + [((o4 + 2, False, c_mla), (o4 + 3, False, c_mla))]
            v_kinds = _plan([(False, 1.0)] * (2 * h))
            k_none = [None] * h
            qtx = _proj(p1x, MLA_Q_RANK, 0, mla_q_norm[j], None, None, 0, wt=wqt, kinds_t=q_kinds, cst=cst)
            kvx, vtx = _proj(p1x, MLA_KV_RANK, kv_blk, mla_kv_norm[j], None, None, 0, w=wk, kinds=k_none,
                             wt=wvt, kinds_t=v_kinds)
            kvc, vtc = _proj(p1c, MLA_KV_RANK, kv_blk, mla_kv_norm[j], None, None, 0, w=wk, kinds=k_none,
                             wt=wvt, kinds_t=v_kinds)
            kvc, p1c_b = per_batch(kvc), per_batch(p1c)
            ox = _mla_attn(qtx, kvc, p1c_b, vtc, kvx, p1x, vtx)
            w_o = mla_w_o[j].astype(BF16)
            if not last:
                qtc = _proj(p1c, MLA_Q_RANK, 0, mla_q_norm[j], None, None, 0, wt=wqt, kinds_t=q_plain)
                oc = _mla_attn(qtc, kvc, p1c_b, vtc)
        elif kind == 1:
            wqkv = swa_w_qkv[j].astype(BF16)
            nq = SWA_KV_HEADS * SWA_GROUP * HEAD_DIM
            nk = SWA_KV_HEADS * HEAD_DIM
            perm = _swa_q_perm()
            wqt = wqkv[:, :nq][:, perm].T
            wk, wvt = wqkv[:, nq:nq + nk], wqkv[:, nq + nk:].T
            gq, gv = nq // HEAD_DIM, nk // HEAD_DIM
            wt = jnp.concatenate([wqt, wvt], axis=0)
            ktx = ([((g, True, c64), (gq + g, False, c64)) for g in range(gq)]
                   + [((2 * gq + g, False, 1.0),) for g in range(gv)])
            kx, ptx = _proj(x, d, 0, g1, ml, x_row, 3, w=wk, kinds=[True] * (nk // LANES), cs=cs,
                            wt=wt, kinds_t=ktx, cst=cst)
            kc, ptc = _proj(xc, d, 0, g1, ml, c_row, 3, w=wk, kinds=[None] * (nk // LANES),
                            wt=wt, kinds_t=_plan([(False, c64)] * gq + [(False, 1.0)] * gv))
            kc = per_batch(kc)
            ox = _swa_attn(swa_sink[j], kc, ptc, kx, ptx)
            w_o = swa_w_o[j].astype(BF16)[perm, :]
            if not last:
                oc = _swa_attn(swa_sink[j], kc, ptc)
        elif kind == 2:
            wqkv = na_w_qkv[j].astype(BF16)
            nh = NA_HEADS * HEAD_DIM
            wk = wqkv[:, nh:2 * nh]
            wt = jnp.concatenate([wqkv[:, :nh], wqkv[:, 2 * nh:]], axis=1).T
            kt = _plan([(False, c64)] * NA_HEADS + [(False, 1.0)] * NA_HEADS)
            k_none = [None] * (nh // LANES)
            kx, ptx = _proj(x, d, 0, g1, ml, x_row, 3, w=wk, kinds=k_none, wt=wt, kinds_t=kt)
            kc, ptc = _proj(xc, d, 0, g1, ml, c_row, 3, w=wk, kinds=k_none, wt=wt, kinds_t=kt)
            kc = per_batch(kc)
            ox = _na_attn(kc, ptc, _na_bias_table(na_rpb[j], seq // GRID_W), kx, ptx)
            w_o = na_w_o[j].astype(BF16)
            if not last:
                oc = _na_attn(kc, ptc)
        else:
            h = DIFF_HEADS
            lam_init = 0.8 - 0.6 * math.exp(-0.3 * layer)
            wqkv = diff_w_qkv[j].astype(BF16)
            nq = 2 * h * HEAD_DIM
            wk = wqkv[:, nq:2 * nq]
            wqt = wqkv[:, :nq].T
            wvt = wqkv[:, 2 * nq:].T
            wt = jnp.concatenate([wqt, wvt], axis=0)
            ktx = ([((4 * (g // 2) + g % 2, True, c64), (4 * (g // 2) + 2 + g % 2, False, c64))
                    for g in range(2 * h)] + [((4 * h + g, False, 1.0),) for g in range(2 * h)])
            ktc = _plan([(False, c64)] * (2 * h) + [(False, 1.0)] * (2 * h))
            kx, ptx = _proj(x, d, 0, g1, ml, x_row, 3, w=wk, kinds=[True] * h, cs=cs, wt=wt, kinds_t=ktx, cst=cst)
            kc, ptc = _proj(xc, d, 0, g1, ml, c_row, 3, w=wk, kinds=[None] * h, wt=wt, kinds_t=ktc)
            kc = per_batch(kc)
            ox = _diff_attn(kc, ptc, diff_lambda[j], diff_norm_g[j], lam_init, 0, h, kx, ptx)
            w_o = diff_w_o[j].astype(BF16)
            if not last:
                oc = _diff_attn(kc, ptc, diff_lambda[j], diff_norm_g[j], lam_init, 0, h)

        x = _ffn(x, norm_g[layer, 2], ml, x_row, w_in, w_out, layer, 1, 6,
                 final_g=final_norm_g if last else None, mixer=(ox, w_o, 5))
        if not last:
            oc = oc.reshape(1, bsz * n_ctx, oc.shape[-1])
            xc = _ffn(xc, norm_g[layer, 2], ml, c_row, w_in, w_out, layer, 1, 6, mixer=(oc, w_o, 5))
    return x
```

```python
import functools
import math

import jax
import jax.numpy as jnp
import numpy as np
from jax import lax
from jax.experimental import pallas as pl
from jax.experimental.pallas import tpu as pltpu

F32 = jnp.float32
BF16 = jnp.bfloat16

GRID_W = 64
N_MOD = 9
EPS = 1e-6
ROPE_BASE = 10000.0
ROPE_DIM = 64
NEG_INF = -1e30
LOG2E = math.log2(math.e)

MLA_HEADS = 8
MLA_Q_RANK = 512
MLA_KV_RANK = 256
MLA_NOPE = 128
MLA_V = 128

SWA_KV_HEADS = 4
SWA_GROUP = 4
SWA_WINDOW = 128

NA_HEADS = 16
NA_KH = 8
NA_KW = 16
NA_STEP_ROWS = 2
NA_WIN_ROWS = NA_KH + NA_STEP_ROWS
NA_CLASSES = 5

DIFF_HEADS = 8
HEAD_DIM = 64

LANES = 128
MXU_WIDTH = 256
VMEM_LIMIT_BYTES = 56 * 1024 * 1024
MOD_ROWS = 8


def _cparams(n_axes):
    return pltpu.CompilerParams(
        dimension_semantics=("arbitrary",) * n_axes,
        vmem_limit_bytes=VMEM_LIMIT_BYTES,
    )


def _resident(shape, index_map):
    return pl.BlockSpec(shape, index_map, pipeline_mode=pl.Buffered(1))


def _rms(x):
    return x * lax.rsqrt(jnp.mean(x * x, axis=-1, keepdims=True) + EPS)


def _norm_mod(x, g, mod_ref, base):
    y = _rms(x) * g
    if mod_ref is not None:
        y = y * (1.0 + mod_ref[base + 1:base + 2, :]) + mod_ref[base:base + 1, :]
    return y


def _dot(a, b):
    return jnp.dot(a, b, preferred_element_type=F32)


def _dot_nt(a, b):
    return lax.dot_general(a, b, (((1,), (1,)), ((), ())), preferred_element_type=F32)


def _row_is_low(shape):
    return lax.broadcasted_iota(jnp.int32, shape, 0) < HEAD_DIM


def _stack_heads_t(qt):
    low = _row_is_low(qt.shape)
    zero = jnp.zeros_like(qt)
    return jnp.concatenate([jnp.where(low, qt, zero), jnp.where(low, zero, qt)], axis=1)


def _rope_tile(y, c, s):
    lane = lax.broadcasted_iota(jnp.int32, y.shape, 1)
    first = (lane & (HEAD_DIM - 1)) < HEAD_DIM // 2
    swapped = jnp.where(first, pltpu.roll(y, LANES - HEAD_DIM // 2, 1), pltpu.roll(y, HEAD_DIM // 2, 1))
    return y * c + swapped * s


class _SoftmaxT:
    def __init__(self, m, l, acc):
        self.m, self.l, self.acc = m, l, acc

    @staticmethod
    def first(s, vt):
        m = jnp.max(s, axis=0, keepdims=True)
        p = jnp.exp2(s - m)
        return _SoftmaxT(m, jnp.sum(p, axis=0, keepdims=True), _dot(vt, p.astype(vt.dtype)))

    def update(self, s, vt):
        m_new = jnp.maximum(self.m, jnp.max(s, axis=0, keepdims=True))
        a = jnp.exp2(self.m - m_new)
        p = jnp.exp2(s - m_new)
        return _SoftmaxT(m_new, a * self.l + jnp.sum(p, axis=0, keepdims=True),
                         a * self.acc + _dot(vt, p.astype(vt.dtype)))


def _attend(qt_ctx, kc, vtc, bufs, qt_lat=None, k_ref=None, vt_ref=None, tk=0):
    c = kc.shape[0]
    n = 1 + (k_ref.shape[0] // tk if k_ref is not None else 0)
    width = qt_ctx.shape[1]
    n_split = 2 if width % (2 * MXU_WIDTH) == 0 else 1
    cols = [slice(u * width // n_split, (u + 1) * width // n_split) for u in range(n_split)]

    def write_scores(j, u):
        if j == 0:
            bufs[0][0:c, cols[u]] = _dot(kc, qt_ctx[:, cols[u]])
        else:
            bufs[j % 2][0:tk, cols[u]] = _dot(k_ref[(j - 1) * tk:j * tk, :], qt_lat[:, cols[u]])

    for u in range(n_split):
        write_scores(0, u)
    sts = [None] * n_split
    for j in range(n):
        for u in range(n_split):
            if j + 1 < n:
                write_scores(j + 1, u)
            if j == 0:
                sts[u] = _SoftmaxT.first(bufs[0][0:c, cols[u]], vtc)
            else:
                sts[u] = sts[u].update(bufs[j % 2][0:tk, cols[u]], vt_ref[:, (j - 1) * tk:j * tk])
    if n_split == 1:
        return sts[0]
    return _SoftmaxT(*(jnp.concatenate([getattr(st, f) for st in sts], axis=1) for f in ("m", "l", "acc")))


def _mod_body(c_ref, w_ref, b_ref, o_ref):
    c = c_ref[...]
    h = (c * jax.nn.sigmoid(c)).astype(BF16)
    o_ref[...] = _dot(h, w_ref[...].astype(BF16)) + b_ref[...]


def _mod_call(cond, mod_w, mod_b):
    depth, d, n = mod_w.shape
    tn = 1024
    return pl.pallas_call(
        _mod_body,
        grid=(depth, n // tn),
        in_specs=[
            pl.BlockSpec((MOD_ROWS, d), lambda l, j: (0, 0)),
            pl.BlockSpec((None, d, tn), lambda l, j: (l, 0, j)),
            pl.BlockSpec((None, 1, tn), lambda l, j: (l, 0, j)),
        ],
        out_specs=pl.BlockSpec((None, MOD_ROWS, tn), lambda l, j: (l, 0, j)),
        out_shape=jax.ShapeDtypeStruct((depth, MOD_ROWS, n), F32),
        compiler_params=_cparams(2),
        name="adaln_mod",
    )(cond, mod_w, mod_b.reshape(depth, 1, n))


def _ffn_body(x_ref, g_ref, mod_ref, win_ref, wout_ref, *rest, base, ff, chunk, final, mixer_base):
    rest = list(rest)
    a_ref, wo_ref = (rest.pop(0), rest.pop(0)) if mixer_base is not None else (None, None)
    fg_ref = rest.pop(0) if final else None
    o_ref, act_ref = rest
    x = x_ref[...]
    if mixer_base is not None:
        x = x + mod_ref[mixer_base:mixer_base + 1, :] * _dot(a_ref[...], wo_ref[...])
    hb = _norm_mod(x, g_ref[...], mod_ref, base).astype(BF16)
    for c0 in range(0, ff, chunk):
        g = _dot(hb, win_ref[:, c0:c0 + chunk])
        u = _dot(hb, win_ref[:, ff + c0:ff + c0 + chunk])
        act_ref[:, c0:c0 + chunk] = (g * jax.nn.sigmoid(g) * u).astype(BF16)
    y = x + (0.5 * mod_ref[base + 2:base + 3, :]) * _dot(act_ref[...], wout_ref[...])
    if final:
        y = _rms(y) * fg_ref[...]
    o_ref[...] = y


def _ffn(x, g, mods_l, mod_row, w_in, w_out, layer, half, base, final_g=None, mixer=None):
    bx, n, d = x.shape
    ff = w_out.shape[2]
    tm = min(512, n)
    final = final_g is not None
    in_specs = [
        pl.BlockSpec((None, tm, d), lambda b, i: (b, i, 0)),
        pl.BlockSpec((1, d), lambda b, i: (0, 0)),
        pl.BlockSpec((None, N_MOD, d), lambda b, i: (mod_row(b), 0, 0)),
        _resident((None, None, d, 2 * ff), lambda b, i: (layer, half, 0, 0)),
        _resident((None, None, ff, d), lambda b, i: (layer, half, 0, 0)),
    ]
    args = [x, g.reshape(1, d), mods_l, w_in, w_out]
    mixer_base = None
    if mixer is not None:
        a, w_o, mixer_base = mixer
        k = a.shape[-1]
        in_specs += [pl.BlockSpec((None, tm, k), lambda b, i: (b, i, 0)), _resident((k, d), lambda b, i: (0, 0))]
        args += [a, w_o]
    if final:
        in_specs.append(pl.BlockSpec((1, d), lambda b, i: (0, 0)))
        args.append(final_g.reshape(1, d))
    return pl.pallas_call(
        functools.partial(_ffn_body, base=base, ff=ff, chunk=MXU_WIDTH, final=final, mixer_base=mixer_base),
        grid=(bx, n // tm),
        in_specs=in_specs,
        out_specs=pl.BlockSpec((None, tm, d), lambda b, i: (b, i, 0)),
        out_shape=jax.ShapeDtypeStruct(x.shape, F32),
        scratch_shapes=[pltpu.VMEM((tm, ff), BF16)],
        compiler_params=_cparams(2),
        name="ffn_swiglu",
    )(*args)


PROJ_T_ROWS = 512


def _proj_body(*refs, base, use_mod, kinds, kinds_t):
    it = iter(refs)
    x_ref, g_ref = next(it), next(it)
    mod_ref = next(it) if use_mod else None
    w_ref = next(it) if kinds else None
    cs_ref = next(it) if any(k is not None for k in kinds) else None
    wt_ref = next(it) if kinds_t else None
    cst_ref = next(it) if _plan_ropes(kinds_t) else None
    o_ref = next(it) if kinds else None
    ot_ref = next(it) if kinds_t else None
    hb = _norm_mod(x_ref[...].astype(F32), g_ref[...], mod_ref, base).astype(BF16)

    n_tiles = len(kinds)
    t = 0
    while t < n_tiles:
        w = min(MXU_WIDTH // LANES, n_tiles - t)
        y = _dot(hb, w_ref[:, t * LANES:(t + w) * LANES])
        for j in range(w):
            yj = y[:, j * LANES:(j + 1) * LANES]
            if kinds[t + j] is not None:
                yj = _rope_tile(yj, cs_ref[0], cs_ref[1])
            o_ref[:, (t + j) * LANES:(t + j + 1) * LANES] = yj.astype(o_ref.dtype)
        t += w

    half = HEAD_DIM // 2
    n_rows_t = len(kinds_t) * HEAD_DIM
    for r0 in range(0, n_rows_t, PROJ_T_ROWS):
        rows = min(PROJ_T_ROWS, n_rows_t - r0)
        yt = _dot_nt(wt_ref[r0:r0 + rows, :], hb)
        for gi in range(rows // HEAD_DIM):
            src = yt[gi * HEAD_DIM:(gi + 1) * HEAD_DIM]
            for dst, rope, scale in kinds_t[r0 // HEAD_DIM + gi]:
                tile = src
                if rope:
                    swapped = jnp.concatenate([src[half:], src[:half]], axis=0)
                    tile = src * cst_ref[0] + swapped * cst_ref[1]
                if scale != 1.0:
                    tile = tile * scale
                ot_ref[dst * HEAD_DIM:(dst + 1) * HEAD_DIM, :] = tile.astype(ot_ref.dtype)


def _plan_ropes(plan):
    return any(rope for outs in plan for _, rope, _ in outs)


def _plan(groups):
    return [((g, rope, scale),) for g, (rope, scale) in enumerate(groups)]


def _proj(x, in_width, in_block, g, mods_l, mod_row, base, w=None, kinds=(), cs=None, out_dtype=BF16,
          wt=None, kinds_t=(), cst=None):
    bx, n, _ = x.shape
    tm = min(512, n)
    use_mod = mods_l is not None
    kinds, kinds_t = tuple(kinds), tuple(kinds_t)
    in_specs = [
        pl.BlockSpec((None, tm, in_width), lambda b, i: (b, i, in_block)),
        pl.BlockSpec((1, in_width), lambda b, i: (0, 0)),
    ]
    args = [x, g.reshape(1, in_width)]
    out_specs, out_shape = [], []
    if use_mod:
        in_specs.append(pl.BlockSpec((None, N_MOD, in_width), lambda b, i: (mod_row(b), 0, 0)))
        args.append(mods_l)
    if kinds:
        f = w.shape[1]
        assert f == len(kinds) * LANES
        in_specs.append(_resident((in_width, f), lambda b, i: (0, 0)))
        args.append(w)
        if any(k is not None for k in kinds):
            in_specs.append(pl.BlockSpec((2, tm, LANES), lambda b, i: (0, i, 0)))
            args.append(cs)
        out_specs.append(pl.BlockSpec((None, tm, f), lambda b, i: (b, i, 0)))
        out_shape.append(jax.ShapeDtypeStruct((bx, n, f), out_dtype))
    if kinds_t:
        fw = wt.shape[0]
        assert fw == len(kinds_t) * HEAD_DIM
        dsts = sorted(dst for outs in kinds_t for dst, _, _ in outs)
        assert dsts == list(range(len(dsts)))
        ft = len(dsts) * HEAD_DIM
        in_specs.append(_resident((fw, in_width), lambda b, i: (0, 0)))
        args.append(wt)
        if _plan_ropes(kinds_t):
            in_specs.append(pl.BlockSpec((2, HEAD_DIM, tm), lambda b, i: (0, 0, i)))
            args.append(cst)
        out_specs.append(pl.BlockSpec((None, ft, tm), lambda b, i: (b, 0, i)))
        out_shape.append(jax.ShapeDtypeStruct((bx, ft, n), BF16))
    outs = pl.pallas_call(
        functools.partial(_proj_body, base=base, use_mod=use_mod, kinds=kinds, kinds_t=kinds_t),
        grid=(bx, n // tm),
        in_specs=in_specs,
        out_specs=out_specs,
        out_shape=out_shape,
        compiler_params=_cparams(2),
        name="norm_proj",
    )(*args)
    return outs[0] if len(outs) == 1 else outs


def _lat_rows(blk):
    return lambda b, h, i: (b, 0, blk(h))


def _lat_t(blk):
    return lambda b, h, i: (b, blk(h), 0)


def _ctx_t(blk):
    return lambda b, h, i: (0, blk(h), b)


def _mla_body(*refs, tk, has_lat):
    if has_lat:
        qt_ref, kn_ref, kr_ref, vt_ref, knc_ref, krc_ref, vtc_ref, o_ref, s_a, s_b, kcat_ref = refs
    else:
        qt_ref, knc_ref, krc_ref, vtc_ref, o_ref, s_a, s_b = refs
        kcat_ref = vt_ref = None
    qt = qt_ref[...]
    kc = jnp.concatenate([knc_ref[...], krc_ref[...].astype(BF16)], axis=1)
    if has_lat:
        @pl.when(pl.program_id(2) == 0)
        def _():
            kcat_ref[:, :LANES] = kn_ref[...]
            kcat_ref[:, LANES:] = kr_ref[...].astype(BF16)

    st = _attend(qt, kc, vtc_ref[...], (s_a, s_b), qt, kcat_ref, vt_ref, tk)
    o_ref[...] = (st.acc / st.l).T.astype(o_ref.dtype)


def _mla_attn(qt, kvc, p1c, vtc, kv=None, p1=None, vt=None):
    h = MLA_HEADS
    has_lat = kv is not None
    bsz, c = kvc.shape[0], kvc.shape[1]
    lq = qt.shape[2] if has_lat else c
    tq = min(1024, lq)
    rope_tile = (MLA_Q_RANK + MLA_KV_RANK) // LANES
    head = lambda hh: hh
    in_specs, args, scratch, tk = [], [qt], [], 0
    if has_lat:
        lk = kv.shape[1]
        tk = min(512, lk)
        in_specs += [
            pl.BlockSpec((None, 2 * LANES, tq), lambda b, hh, i: (b, hh, i)),
            pl.BlockSpec((None, lk, LANES), _lat_rows(head)),
            pl.BlockSpec((None, lk, LANES), _lat_rows(lambda hh: rope_tile)),
            pl.BlockSpec((None, LANES, lk), _lat_t(head)),
        ]
        args += [kv, p1, vt]
        scratch = [pltpu.VMEM((lk, 2 * LANES), BF16)]
    else:
        in_specs.append(pl.BlockSpec((None, 2 * LANES, tq), _ctx_t(head)))
    in_specs += [
        pl.BlockSpec((None, c, LANES), _lat_rows(head)),
        pl.BlockSpec((None, c, LANES), _lat_rows(lambda hh: rope_tile)),
        pl.BlockSpec((None, LANES, c), _ctx_t(head)),
    ]
    args += [kvc, p1c, vtc]
    return pl.pallas_call(
        functools.partial(_mla_body, tk=tk, has_lat=has_lat),
        grid=(bsz, h, lq // tq),
        in_specs=in_specs,
        out_specs=pl.BlockSpec((None, tq, LANES), lambda b, hh, i: (b, i, hh)),
        out_shape=jax.ShapeDtypeStruct((bsz, lq, h * MLA_V), BF16),
        scratch_shapes=[pltpu.VMEM((max(tk, c), tq), F32)] * 2 + scratch,
        compiler_params=_cparams(3),
        name="mla_attention",
    )(*args)


def _diff_body(*refs, lam_init, tk, has_lat):
    if has_lat:
        qt_ref, k_ref, vt_ref, kc_ref, vtc_ref, lam_ref, ng_ref, o_ref, s_a, s_b = refs
        q_plain, q_rot = _stack_heads_t(qt_ref[LANES:, :]), _stack_heads_t(qt_ref[:LANES, :])
    else:
        qt_ref, kc_ref, vtc_ref, lam_ref, ng_ref, o_ref, s_a, s_b = refs
        q_plain, q_rot, k_ref, vt_ref = _stack_heads_t(qt_ref[...]), None, None, None
    tq = qt_ref.shape[1]
    st = _attend(q_plain, kc_ref[...], vtc_ref[...], (s_a, s_b), q_rot, k_ref, vt_ref, tk)
    lp = lam_ref[...]
    lam = (jnp.exp(jnp.sum(lp[0:1] * lp[1:2], axis=-1, keepdims=True))
           - jnp.exp(jnp.sum(lp[2:3] * lp[3:4], axis=-1, keepdims=True)) + lam_init)
    o = st.acc / st.l
    o = (o[:, :tq] - lam * o[:, tq:]).T
    o_ref[...] = (_rms(o) * ng_ref[...] * (1.0 - lam_init)).astype(o_ref.dtype)


def _diff_attn(kc, ptc, lam_params, norm_g, lam_init, qtc_blk, vtc_blk, kx=None, ptx=None):
    h = DIFF_HEADS
    has_lat = kx is not None
    bsz, c = kc.shape[0], kc.shape[1]
    lq = kx.shape[1] if has_lat else c
    tq = min(512, lq)
    head = lambda hh: hh
    in_specs, args, tk = [], [], 0
    if has_lat:
        lk = kx.shape[1]
        tk = min(512, lk)
        in_specs += [
            pl.BlockSpec((None, 2 * LANES, tq), lambda b, hh, i: (b, hh, i)),
            pl.BlockSpec((None, lk, LANES), _lat_rows(head)),
            pl.BlockSpec((None, LANES, lk), _lat_t(lambda hh: 2 * h + hh)),
        ]
        args += [ptx, kx, ptx]
    else:
        in_specs.append(pl.BlockSpec((None, LANES, tq), _ctx_t(lambda hh: qtc_blk + hh)))
        args.append(ptc)
    in_specs += [
        pl.BlockSpec((None, c, LANES), _lat_rows(head)),
        pl.BlockSpec((None, LANES, c), _ctx_t(lambda hh: vtc_blk + hh)),
        pl.BlockSpec((4, HEAD_DIM), lambda b, hh, i: (0, 0)),
        pl.BlockSpec((1, 2 * HEAD_DIM), lambda b, hh, i: (0, 0)),
    ]
    args += [kc, ptc, lam_params, norm_g.reshape(1, 2 * HEAD_DIM)]
    return pl.pallas_call(
        functools.partial(_diff_body, lam_init=lam_init, tk=tk, has_lat=has_lat),
        grid=(bsz, h, lq // tq),
        in_specs=in_specs,
        out_specs=pl.BlockSpec((None, tq, LANES), lambda b, hh, i: (b, i, hh)),
        out_shape=jax.ShapeDtypeStruct((bsz, lq, h * 2 * HEAD_DIM), BF16),
        scratch_shapes=[pltpu.VMEM((max(tk, c), 2 * tq), F32)] * 2,
        compiler_params=_cparams(3),
        name="diff_attention",
    )(*args)


def _swa_body(*refs, tq, has_lat):
    if has_lat:
        sink_ref, qr_ref, qp_ref, k_ref, vt_ref, kc_ref, vtc_ref, o_ref, s_a, s_b = refs
    else:
        sink_ref, qp_ref, kc_ref, vtc_ref, o_ref, s_a, s_b = refs
    bufs = (s_a, s_b)
    pair = pl.program_id(1)
    kc, vtc = kc_ref[...], vtc_ref[...]
    c = kc.shape[0]
    span = tq + 2 * SWA_WINDOW if has_lat else 0
    first_half = lax.broadcasted_iota(jnp.int32, (1, 2 * tq), 1) < tq
    if has_lat:
        lk = k_ref.shape[0]
        q0 = pl.program_id(2) * tq
        start = pl.multiple_of(jnp.clip(q0 - SWA_WINDOW, 0, lk - span), SWA_WINDOW)
        keys = pl.ds(start, span)
        kpos = start + lax.broadcasted_iota(jnp.int32, (span, 2 * tq), 0)
        qpos = q0 + (lax.broadcasted_iota(jnp.int32, (span, 2 * tq), 1) & (tq - 1))
        valid = jnp.abs(kpos - qpos) <= SWA_WINDOW
    low = _row_is_low((LANES, tq))

    def scores(g):
        rows = slice(g * LANES, (g + 1) * LANES)
        buf = bufs[g % 2]
        buf[0:c, :] = _dot(kc, _stack_heads_t(qp_ref[rows, :]))
        if has_lat:
            s_w = _dot(k_ref[keys, :], _stack_heads_t(qr_ref[rows, :]))
            buf[c:c + span, :] = jnp.where(valid, s_w, NEG_INF)

    scores(0)
    for g in range(SWA_GROUP):
        if g + 1 < SWA_GROUP:
            scores(g + 1)
        sink = jnp.where(first_half, sink_ref[2 * pair * SWA_GROUP + g],
                         sink_ref[(2 * pair + 1) * SWA_GROUP + g]) * LOG2E
        s = bufs[g % 2][...]
        m = jnp.maximum(jnp.max(s, axis=0, keepdims=True), sink)
        p = jnp.exp2(s - m)
        l = jnp.sum(p, axis=0, keepdims=True) + jnp.exp2(sink - m)
        p = p.astype(BF16)
        o = _dot(vtc, p[0:c])
        if has_lat:
            o = o + _dot(vt_ref[:, keys], p[c:])
        o = o / l
        o_ref[:, g * LANES:(g + 1) * LANES] = jnp.where(low, o[:, :tq], o[:, tq:]).T.astype(o_ref.dtype)


def _swa_attn(sink, kc, ptc, kx=None, ptx=None):
    n_pairs = SWA_KV_HEADS // 2
    qw = SWA_GROUP * LANES
    has_lat = kx is not None
    bsz, c = kc.shape[0], kc.shape[1]
    lq = kx.shape[1] if has_lat else c
    tq = min(256, lq)
    pair = lambda p: p
    in_specs = [pl.BlockSpec(memory_space=pltpu.SMEM)]
    args = [sink]
    span = 0
    if has_lat:
        lk = kx.shape[1]
        span = tq + 2 * SWA_WINDOW
        in_specs += [
            pl.BlockSpec((None, qw, tq), lambda b, p, i: (b, p, i)),
            pl.BlockSpec((None, qw, tq), lambda b, p, i: (b, n_pairs + p, i)),
            pl.BlockSpec((None, lk, LANES), _lat_rows(pair)),
            pl.BlockSpec((None, LANES, lk), _lat_t(lambda p: 2 * n_pairs * SWA_GROUP + p)),
        ]
        args += [ptx, ptx, kx, ptx]
    else:
        in_specs.append(pl.BlockSpec((None, qw, tq), _ctx_t(pair)))
        args.append(ptc)
    in_specs += [
        pl.BlockSpec((None, c, LANES), _lat_rows(pair)),
        pl.BlockSpec((None, LANES, c), _ctx_t(lambda p: n_pairs * SWA_GROUP + p)),
    ]
    args += [kc, ptc]
    return pl.pallas_call(
        functools.partial(_swa_body, tq=tq, has_lat=has_lat),
        grid=(bsz, n_pairs, lq // tq),
        in_specs=in_specs,
        out_specs=pl.BlockSpec((None, tq, qw), lambda b, p, i: (b, i, p)),
        out_shape=jax.ShapeDtypeStruct((bsz, lq, n_pairs * qw), BF16),
        scratch_shapes=[pltpu.VMEM((c + span, 2 * tq), F32)] * 2,
        compiler_params=_cparams(3),
        name="swa_attention",
    )(*args)


def _na_body(*refs, n_rows, has_lat):
    if has_lat:
        qt_ref, k_ref, vt_ref, bias_ref, kc_ref, vtc_ref, o_ref, s_a, s_b = refs
    else:
        qt_ref, kc_ref, vtc_ref, o_ref, s_a, s_b = refs
    bufs = (s_a, s_b)
    kc, vtc = kc_ref[...], vtc_ref[...]
    c = kc.shape[0]
    n_steps = qt_ref.shape[1] // LANES
    win = NA_WIN_ROWS * GRID_W
    low = _row_is_low((LANES, LANES))

    def scores(u):
        q = _stack_heads_t(qt_ref[:, u * LANES:(u + 1) * LANES])
        buf = bufs[u % 2]
        buf[0:c, :] = _dot(kc, q)
        if not has_lat:
            return None
        r = (pl.program_id(2) * n_steps + u) * NA_STEP_ROWS
        ws = jnp.clip(r - NA_KH // 2, 0, n_rows - NA_WIN_ROWS)
        cls = lax.shift_right_logical(r - ws, 1)
        keys = pl.ds(pl.multiple_of(ws * GRID_W, LANES), win)
        buf[c:c + win, :] = _dot(k_ref[keys, :], q) + bias_ref[cls]
        return keys

    keys = scores(0)
    for u in range(n_steps):
        next_keys = scores(u + 1) if u + 1 < n_steps else None
        s = bufs[u % 2][...]
        p = jnp.exp2(s - jnp.max(s, axis=0, keepdims=True))
        l = jnp.sum(p, axis=0, keepdims=True)
        p = p.astype(BF16)
        o = _dot(vtc, p[0:c])
        if has_lat:
            o = o + _dot(vt_ref[:, keys], p[c:])
        o = o / l
        o_ref[u * LANES:(u + 1) * LANES, :] = jnp.where(low, o[:, :LANES], o[:, LANES:]).T.astype(o_ref.dtype)
        keys = next_keys


def _na_attn(kc, ptc, bias=None, kx=None, ptx=None):
    n_pairs = NA_HEADS // 2
    has_lat = kx is not None
    bsz, c = kc.shape[0], kc.shape[1]
    pair = lambda p: p
    in_specs, args, n_rows = [], [], 0
    if has_lat:
        lq = kx.shape[1]
        n_rows = lq // GRID_W
        tq = min(1024, lq)
        in_specs += [
            pl.BlockSpec((None, LANES, tq), lambda b, p, i: (b, p, i)),
            pl.BlockSpec((None, lq, LANES), _lat_rows(pair)),
            pl.BlockSpec((None, LANES, lq), _lat_t(lambda p: n_pairs + p)),
            pl.BlockSpec((None, NA_CLASSES, NA_WIN_ROWS * GRID_W, 2 * LANES), lambda b, p, i: (p, 0, 0, 0)),
        ]
        args += [ptx, kx, ptx, bias]
    else:
        lq = tq = c
        in_specs.append(pl.BlockSpec((None, LANES, tq), _ctx_t(pair)))
        args.append(ptc)
    in_specs += [
        pl.BlockSpec((None, c, LANES), _lat_rows(pair)),
        pl.BlockSpec((None, LANES, c), _ctx_t(lambda p: n_pairs + p)),
    ]
    args += [kc, ptc]
    return pl.pallas_call(
        functools.partial(_na_body, n_rows=n_rows, has_lat=has_lat),
        grid=(bsz, n_pairs, lq // tq),
        in_specs=in_specs,
        out_specs=pl.BlockSpec((None, tq, LANES), lambda b, p, i: (b, i, p)),
        out_shape=jax.ShapeDtypeStruct((bsz, lq, NA_HEADS * HEAD_DIM), BF16),
        scratch_shapes=[pltpu.VMEM((c + (NA_WIN_ROWS * GRID_W if has_lat else 0), 2 * LANES), F32)] * 2,
        compiler_params=_cparams(3),
        name="na_attention",
    )(*args)


def _rope_tables(seq):
    t = jnp.arange(seq, dtype=jnp.int32)
    row = (t // GRID_W).astype(F32)
    col = (t % GRID_W).astype(F32)
    n_freq = ROPE_DIM // 4
    inv = jnp.exp(-math.log(ROPE_BASE) * jnp.arange(n_freq, dtype=F32) / n_freq)
    ang = jnp.concatenate([row[:, None] * inv, col[:, None] * inv], axis=-1)
    cos, sin = jnp.cos(ang), jnp.sin(ang)
    c_head = jnp.concatenate([cos, cos], axis=-1)
    s_head = jnp.concatenate([-sin, sin], axis=-1)
    cs = jnp.stack([jnp.concatenate([c_head, c_head], axis=-1), jnp.concatenate([s_head, s_head], axis=-1)])
    cst = jnp.stack([c_head.T, s_head.T])
    return cs, cst


def _na_bias_table(rpb, n_rows):
    assert n_rows % 2 == 0 and n_rows >= NA_WIN_ROWS + 2 and NA_STEP_ROWS == 2
    rel_start = np.array([[0, 0], [0, 0], [0, 1], [2, 2], [2, 2]])
    dq = np.array([[0, 1], [2, 3], [4, 5], [6, 7], [8, 9]])
    jr = np.arange(NA_WIN_ROWS)
    row_ok = (jr >= rel_start[:, :, None]) & (jr < rel_start[:, :, None] + NA_KH)
    ridx = np.clip(jr - dq[:, :, None] + (NA_KH - 1), 0, 2 * NA_KH - 2)
    n_pairs = rpb.shape[0] // 2
    rows = jnp.where(row_ok[None, :, :, :, None], rpb.astype(F32)[:, ridx] * LOG2E, NEG_INF)
    pad = GRID_W - NA_KW
    rows = jnp.pad(rows[..., ::-1], [(0, 0)] * 4 + [(pad, pad + 1)])
    rows = rows.reshape(n_pairs, 2, NA_CLASSES, NA_STEP_ROWS, NA_WIN_ROWS, LANES)
    rows = rows.transpose(0, 2, 1, 3, 4, 5).reshape(n_pairs, NA_CLASSES, -1, LANES)
    return pl.pallas_call(
        _na_bias_body,
        grid=(n_pairs, NA_CLASSES),
        in_specs=[pl.BlockSpec((None, None, rows.shape[2], LANES), lambda p, c: (p, c, 0, 0))],
        out_specs=pl.BlockSpec((None, None, NA_WIN_ROWS * GRID_W, 2 * LANES), lambda p, c: (p, c, 0, 0)),
        out_shape=jax.ShapeDtypeStruct((n_pairs, NA_CLASSES, NA_WIN_ROWS * GRID_W, 2 * LANES), F32),
        compiler_params=_cparams(2),
        name="na_bias_table",
    )(rows)


def _na_bias_body(r_ref, o_ref):
    kcol = lax.broadcasted_iota(jnp.int32, (GRID_W, LANES), 0)
    lane = lax.broadcasted_iota(jnp.int32, (GRID_W, LANES), 1)
    col = lane & (GRID_W - 1)
    col_start = jnp.clip(col - NA_KW // 2, 0, GRID_W - NA_KW)
    in_win = (kcol >= col_start) & (kcol < col_start + NA_KW)
    low = lane < GRID_W
    for e in range(2):
        for jr in range(NA_WIN_ROWS):
            halves = []
            for rr in range(NA_STEP_ROWS):
                t = (e * NA_STEP_ROWS + rr) * NA_WIN_ROWS + jr
                v = jnp.broadcast_to(r_ref[t:t + 1, :], (GRID_W, LANES))
                shift = LANES - (GRID_W - 1) if rr == 0 else LANES - (2 * GRID_W - 1)
                halves.append(pltpu.roll(v, shift, 1, stride=1, stride_axis=0))
            tile = jnp.where(in_win, jnp.where(low, halves[0], halves[1]), NEG_INF)
            o_ref[jr * GRID_W:(jr + 1) * GRID_W, e * LANES:(e + 1) * LANES] = tile


def _swa_q_perm():
    idx = []
    for pair in range(SWA_KV_HEADS // 2):
        for g in range(SWA_GROUP):
            for e in range(2):
                head = (2 * pair + e) * SWA_GROUP + g
                idx.extend(range(head * HEAD_DIM, (head + 1) * HEAD_DIM))
    return np.asarray(idx)


def kernel(x, c, ctx, c_ctx, mod_w, mod_b, norm_g, final_norm_g, ffn_w_in, ffn_w_out,
           mla_w_down, mla_q_norm, mla_kv_norm, mla_w_uq, mla_w_ukv, mla_w_o,
           swa_w_qkv, swa_sink, swa_w_o, na_w_qkv, na_rpb, na_w_o,
           diff_w_qkv, diff_lambda, diff_norm_g, diff_w_o):
    bsz, seq, d = x.shape
    n_ctx = ctx.shape[1]
    depth = mod_w.shape[0]
    assert bsz + 1 <= MOD_ROWS and seq % GRID_W == 0 and n_ctx % LANES == 0

    cond = jnp.zeros((MOD_ROWS, d), F32).at[:bsz].set(c).at[bsz].set(c_ctx)
    mods = _mod_call(cond, mod_w, mod_b).reshape(depth, MOD_ROWS, N_MOD, d)
    cs, cst = _rope_tables(seq)
    w_in, w_out = ffn_w_in.astype(BF16), ffn_w_out.astype(BF16)
    xc = ctx.reshape(1, bsz * n_ctx, d)
    x_row = lambda b: b
    c_row = lambda b: bsz
    c64 = HEAD_DIM ** -0.5 * LOG2E

    def per_batch(t):
        return t.reshape(bsz, n_ctx, t.shape[-1])

    for layer in range(depth):
        kind = layer % 4
        j = layer // 4
        last = layer == depth - 1
        ml = mods[layer]

        x = _ffn(x, norm_g[layer, 0], ml, x_row, w_in, w_out, layer, 0, 0)
        xc = _ffn(xc, norm_g[layer, 0], ml, c_row, w_in, w_out, layer, 0, 0)
        g1 = norm_g[layer, 1]

        if kind == 0:
            h = MLA_HEADS
            qk = MLA_Q_RANK + MLA_KV_RANK
            c_mla = (MLA_NOPE + ROPE_DIM) ** -0.5 * LOG2E
            wd = mla_w_down[j].astype(BF16)
            z = jnp.zeros((d, ROPE_DIM), BF16)
            w1x = jnp.concatenate([wd, z], axis=1)
            w1c = jnp.concatenate([wd[:, :qk], z, wd[:, qk:]], axis=1)
            n1 = w1x.shape[1] // LANES
            p1x = _proj(x, d, 0, g1, ml, x_row, 3, w=w1x, kinds=[None] * (n1 - 1) + [True], cs=cs, out_dtype=F32)
            p1c = _proj(xc, d, 0, g1, ml, c_row, 3, w=w1c, kinds=[None] * n1, out_dtype=F32)
            wqt = mla_w_uq[j].astype(BF16).T
            wkv = mla_w_ukv[j].astype(BF16).reshape(MLA_KV_RANK, h, MLA_NOPE + MLA_V)
            wk = wkv[:, :, :MLA_NOPE].reshape(MLA_KV_RANK, -1)
            wvt = wkv[:, :, MLA_NOPE:].reshape(MLA_KV_RANK, -1).T
            kv_blk = MLA_Q_RANK // MLA_KV_RANK
            q_kinds, q_plain = [], []
            for hh in range(h):
                o4 = 4 * hh
                nope = [((o4, False, c_mla),), ((o4 + 1, False, c_mla),)]
                q_kinds += nope + [((o4 + 2, True, c_mla), (o4 + 3, False, c_mla))]
                q_plain += nope + [((o4 + 2, False, c_mla), (o4 + 3, False, c_mla))]
            v_kinds = _plan([(False, 1.0)] * (2 * h))
            k_none = [None] * h
            qtx = _proj(p1x, MLA_Q_RANK, 0, mla_q_norm[j], None, None, 0, wt=wqt, kinds_t=q_kinds, cst=cst)
            kvx, vtx = _proj(p1x, MLA_KV_RANK, kv_blk, mla_kv_norm[j], None, None, 0, w=wk, kinds=k_none,
                             wt=wvt, kinds_t=v_kinds)
            kvc, vtc = _proj(p1c, MLA_KV_RANK, kv_blk, mla_kv_norm[j], None, None, 0, w=wk, kinds=k_none,
                             wt=wvt, kinds_t=v_kinds)
            kvc, p1c_b = per_batch(kvc), per_batch(p1c)
            ox = _mla_attn(qtx, kvc, p1c_b, vtc, kvx, p1x, vtx)
            w_o = mla_w_o[j].astype(BF16)
            if not last:
                qtc = _proj(p1c, MLA_Q_RANK, 0, mla_q_norm[j], None, None, 0, wt=wqt, kinds_t=q_plain)
                oc = _mla_attn(qtc, kvc, p1c_b, vtc)
        elif kind == 1:
            wqkv = swa_w_qkv[j].astype(BF16)
            nq = SWA_KV_HEADS * SWA_GROUP * HEAD_DIM
            nk = SWA_KV_HEADS * HEAD_DIM
            perm = _swa_q_perm()
            wqt = wqkv[:, :nq][:, perm].T
            wk, wvt = wqkv[:, nq:nq + nk], wqkv[:, nq + nk:].T
            gq, gv = nq // HEAD_DIM, nk // HEAD_DIM
            wt = jnp.concatenate([wqt, wvt], axis=0)
            ktx = ([((g, True, c64), (gq + g, False, c64)) for g in range(gq)]
                   + [((2 * gq + g, False, 1.0),) for g in range(gv)])
            kx, ptx = _proj(x, d, 0, g1, ml, x_row, 3, w=wk, kinds=[True] * (nk // LANES), cs=cs,
                            wt=wt, kinds_t=ktx, cst=cst)
            kc, ptc = _proj(xc, d, 0, g1, ml, c_row, 3, w=wk, kinds=[None] * (nk // LANES),
                            wt=wt, kinds_t=_plan([(False, c64)] * gq + [(False, 1.0)] * gv))
            kc = per_batch(kc)
            ox = _swa_attn(swa_sink[j], kc, ptc, kx, ptx)
            w_o = swa_w_o[j].astype(BF16)[perm, :]
            if not last:
                oc = _swa_attn(swa_sink[j], kc, ptc)
        elif kind == 2:
            wqkv = na_w_qkv[j].astype(BF16)
            nh = NA_HEADS * HEAD_DIM
            wk = wqkv[:, nh:2 * nh]
            wt = jnp.concatenate([wqkv[:, :nh], wqkv[:, 2 * nh:]], axis=1).T
            kt = _plan([(False, c64)] * NA_HEADS + [(False, 1.0)] * NA_HEADS)
            k_none = [None] * (nh // LANES)
            kx, ptx = _proj(x, d, 0, g1, ml, x_row, 3, w=wk, kinds=k_none, wt=wt, kinds_t=kt)
            kc, ptc = _proj(xc, d, 0, g1, ml, c_row, 3, w=wk, kinds=k_none, wt=wt, kinds_t=kt)
            kc = per_batch(kc)
            ox = _na_attn(kc, ptc, _na_bias_table(na_rpb[j], seq // GRID_W), kx, ptx)
            w_o = na_w_o[j].astype(BF16)
            if not last:
                oc = _na_attn(kc, ptc)
        else:
            h = DIFF_HEADS
            lam_init = 0.8 - 0.6 * math.exp(-0.3 * layer)
            wqkv = diff_w_qkv[j].astype(BF16)
            nq = 2 * h * HEAD_DIM
            wk = wqkv[:, nq:2 * nq]
            wqt = wqkv[:, :nq].T
            wvt = wqkv[:, 2 * nq:].T
            wt = jnp.concatenate([wqt, wvt], axis=0)
            ktx = ([((4 * (g // 2) + g % 2, True, c64), (4 * (g // 2) + 2 + g % 2, False, c64))
                    for g in range(2 * h)] + [((4 * h + g, False, 1.0),) for g in range(2 * h)])
            ktc = _plan([(False, c64)] * (2 * h) + [(False, 1.0)] * (2 * h))
            kx, ptx = _proj(x, d, 0, g1, ml, x_row, 3, w=wk, kinds=[True] * h, cs=cs, wt=wt, kinds_t=ktx, cst=cst)
            kc, ptc = _proj(xc, d, 0, g1, ml, c_row, 3, w=wk, kinds=[None] * h, wt=wt, kinds_t=ktc)
            kc = per_batch(kc)
            ox = _diff_attn(kc, ptc, diff_lambda[j], diff_norm_g[j], lam_init, 0, h, kx, ptx)
            w_o = diff_w_o[j].astype(BF16)
            if not last:
                oc = _diff_attn(kc, ptc, diff_lambda[j], diff_norm_g[j], lam_init, 0, h)

        x = _ffn(x, norm_g[layer, 2], ml, x_row, w_in, w_out, layer, 1, 6,
                 final_g=final_norm_g if last else None, mixer=(ox, w_o, 5))
        if not last:
            oc = oc.reshape(1, bsz * n_ctx, oc.shape[-1])
            xc = _ffn(xc, norm_g[layer, 2], ml, c_row, w_in, w_out, layer, 1, 6, mixer=(oc, w_o, 5))
    return x
```

```python
import functools
import math

import jax
import jax.numpy as jnp
import numpy as np
from jax import lax
from jax.experimental import pallas as pl
from jax.experimental.pallas import tpu as pltpu

F32 = jnp.float32
BF16 = jnp.bfloat16

GRID_W = 64
N_MOD = 9
EPS = 1e-6
ROPE_BASE = 10000.0
ROPE_DIM = 64
NEG_INF = -1e30
LOG2E = math.log2(math.e)

MLA_HEADS = 8
MLA_Q_RANK = 512
MLA_KV_RANK = 256
MLA_NOPE = 128
MLA_V = 128

SWA_KV_HEADS = 4
SWA_GROUP = 4
SWA_WINDOW = 128

NA_HEADS = 16
NA_KH = 8
NA_KW = 16
NA_STEP_ROWS = 2
NA_WIN_ROWS = NA_KH + NA_STEP_ROWS
NA_CLASSES = 5

DIFF_HEADS = 8
HEAD_DIM = 64

LANES = 128
MXU_WIDTH = 256
VMEM_LIMIT_BYTES = 56 * 1024 * 1024
MOD_ROWS = 8


def _cparams(n_axes):
    return pltpu.CompilerParams(
        dimension_semantics=("arbitrary",) * n_axes,
        vmem_limit_bytes=VMEM_LIMIT_BYTES,
    )


def _resident(shape, index_map):
    return pl.BlockSpec(shape, index_map, pipeline_mode=pl.Buffered(1))


def _rms(x):
    return x * lax.rsqrt(jnp.mean(x * x, axis=-1, keepdims=True) + EPS)


def _norm_mod(x, g, mod_ref, base):
    y = _rms(x) * g
    if mod_ref is not None:
        y = y * (1.0 + mod_ref[base + 1:base + 2, :]) + mod_ref[base:base + 1, :]
    return y


def _dot(a, b):
    return jnp.dot(a, b, preferred_element_type=F32)


def _dot_nt(a, b):
    return lax.dot_general(a, b, (((1,), (1,)), ((), ())), preferred_element_type=F32)


def _row_is_low(shape):
    return lax.broadcasted_iota(jnp.int32, shape, 0) < HEAD_DIM


def _stack_heads_t(qt):
    low = _row_is_low(qt.shape)
    zero = jnp.zeros_like(qt)
    return jnp.concatenate([jnp.where(low, qt, zero), jnp.where(low, zero, qt)], axis=1)


def _rope_tile(y, c, s):
    lane = lax.broadcasted_iota(jnp.int32, y.shape, 1)
    first = (lane & (HEAD_DIM - 1)) < HEAD_DIM // 2
    swapped = jnp.where(first, pltpu.roll(y, LANES - HEAD_DIM // 2, 1), pltpu.roll(y, HEAD_DIM // 2, 1))
    return y * c + swapped * s


class _SoftmaxT:
    def __init__(self, m, l, acc):
        self.m, self.l, self.acc = m, l, acc

    @staticmethod
    def first(s, vt):
        m = jnp.max(s, axis=0, keepdims=True)
        p = jnp.exp2(s - m)
        return _SoftmaxT(m, jnp.sum(p, axis=0, keepdims=True), _dot(vt, p.astype(vt.dtype)))

    def update(self, s, vt):
        m_new = jnp.maximum(self.m, jnp.max(s, axis=0, keepdims=True))
        a = jnp.exp2(self.m - m_new)
        p = jnp.exp2(s - m_new)
        return _SoftmaxT(m_new, a * self.l + jnp.sum(p, axis=0, keepdims=True),
                         a * self.acc + _dot(vt, p.astype(vt.dtype)))


def _attend(qt_ctx, kc, vtc, bufs, qt_lat=None, k_ref=None, vt_ref=None, tk=0):
    c = kc.shape[0]
    n = 1 + (k_ref.shape[0] // tk if k_ref is not None else 0)
    width = qt_ctx.shape[1]
    n_split = 2 if width % (2 * MXU_WIDTH) == 0 else 1
    cols = [slice(u * width // n_split, (u + 1) * width // n_split) for u in range(n_split)]

    def write_scores(j, u):
        if j == 0:
            bufs[0][0:c, cols[u]] = _dot(kc, qt_ctx[:, cols[u]])
        else:
            bufs[j % 2][0:tk, cols[u]] = _dot(k_ref[(j - 1) * tk:j * tk, :], qt_lat[:, cols[u]])

    for u in range(n_split):
        write_scores(0, u)
    sts = [None] * n_split
    for j in range(n):
        for u in range(n_split):
            if j + 1 < n:
                write_scores(j + 1, u)
            if j == 0:
                sts[u] = _SoftmaxT.first(bufs[0][0:c, cols[u]], vtc)
            else:
                sts[u] = sts[u].update(bufs[j % 2][0:tk, cols[u]], vt_ref[:, (j - 1) * tk:j * tk])
    if n_split == 1:
        return sts[0]
    return _SoftmaxT(*(jnp.concatenate([getattr(st, f) for st in sts], axis=1) for f in ("m", "l", "acc")))


def _mod_body(c_ref, w_ref, b_ref, o_ref):
    c = c_ref[...]
    h = (c * jax.nn.sigmoid(c)).astype(BF16)
    o_ref[...] = _dot(h, w_ref[...].astype(BF16)) + b_ref[...]


def _mod_call(cond, mod_w, mod_b):
    depth, d, n = mod_w.shape
    tn = 1024
    return pl.pallas_call(
        _mod_body,
        grid=(depth, n // tn),
        in_specs=[
            pl.BlockSpec((MOD_ROWS, d), lambda l, j: (0, 0)),
            pl.BlockSpec((None, d, tn), lambda l, j: (l, 0, j)),
            pl.BlockSpec((None, 1, tn), lambda l, j: (l, 0, j)),
        ],
        out_specs=pl.BlockSpec((None, MOD_ROWS, tn), lambda l, j: (l, 0, j)),
        out_shape=jax.ShapeDtypeStruct((depth, MOD_ROWS, n), F32),
        compiler_params=_cparams(2),
        name="adaln_mod",
    )(cond, mod_w, mod_b.reshape(depth, 1, n))


def _ffn_body(x_ref, g_ref, mod_ref, win_ref, wout_ref, *rest, base, ff, chunk, final, mixer_base, proj):
    it = iter(rest[:-2])
    o_ref, act_ref = rest[-2:]
    a_ref, wo_ref = (next(it), next(it)) if mixer_base is not None else (None, None)
    fg_ref = next(it) if final else None
    x = x_ref[...]
    if mixer_base is not None:
        x = x + mod_ref[mixer_base:mixer_base + 1, :] * _dot(a_ref[...], wo_ref[...])
    hb = _norm_mod(x, g_ref[...], mod_ref, base).astype(BF16)
    for c0 in range(0, ff, chunk):
        g = _dot(hb, win_ref[:, c0:c0 + chunk])
        u = _dot(hb, win_ref[:, ff + c0:ff + c0 + chunk])
        act_ref[:, c0:c0 + chunk] = (g * jax.nn.sigmoid(g) * u).astype(BF16)
    y = x + (0.5 * mod_ref[base + 2:base + 3, :]) * _dot(act_ref[...], wout_ref[...])
    if final:
        y = _rms(y) * fg_ref[...]
    o_ref[...] = y
    if proj is not None:
        proj_base, kinds, kinds_t = proj
        g2_ref = next(it)
        _proj_emit(_norm_mod(y, g2_ref[...], mod_ref, proj_base).astype(BF16), it, kinds, kinds_t)


def _ffn(x, g, mods_l, mod_row, w_in, w_out, layer, half, base, final_g=None, mixer=None, proj=None):
    bx, n, d = x.shape
    ff = w_out.shape[2]
    tm = min(512, n)
    final = final_g is not None
    in_specs = [
        pl.BlockSpec((None, tm, d), lambda b, i: (b, i, 0)),
        pl.BlockSpec((1, d), lambda b, i: (0, 0)),
        pl.BlockSpec((None, N_MOD, d), lambda b, i: (mod_row(b), 0, 0)),
        _resident((None, None, d, 2 * ff), lambda b, i: (layer, half, 0, 0)),
        _resident((None, None, ff, d), lambda b, i: (layer, half, 0, 0)),
    ]
    args = [x, g.reshape(1, d), mods_l, w_in, w_out]
    mixer_base = None
    if mixer is not None:
        a, w_o, mixer_base = mixer
        k = a.shape[-1]
        in_specs += [pl.BlockSpec((None, tm, k), lambda b, i: (b, i, 0)), _resident((k, d), lambda b, i: (0, 0))]
        args += [a, w_o]
    if final:
        in_specs.append(pl.BlockSpec((1, d), lambda b, i: (0, 0)))
        args.append(final_g.reshape(1, d))
    out_specs, out_shape, proj_static = [], [], None
    if proj is not None:
        kinds, kinds_t = tuple(proj.get("kinds", ())), tuple(proj.get("kinds_t", ()))
        p_specs, p_args, out_specs, out_shape = _proj_io(
            bx, n, tm, d, proj.get("w"), kinds, proj.get("cs"), proj.get("out_dtype", BF16),
            proj.get("wt"), kinds_t, proj.get("cst"))
        in_specs += [pl.BlockSpec((1, d), lambda b, i: (0, 0))] + p_specs
        args += [proj["g"].reshape(1, d)] + p_args
        proj_static = (proj["base"], kinds, kinds_t)
    outs = pl.pallas_call(
        functools.partial(_ffn_body, base=base, ff=ff, chunk=MXU_WIDTH, final=final, mixer_base=mixer_base,
                          proj=proj_static),
        grid=(bx, n // tm),
        in_specs=in_specs,
        out_specs=out_specs + [pl.BlockSpec((None, tm, d), lambda b, i: (b, i, 0))],
        out_shape=out_shape + [jax.ShapeDtypeStruct(x.shape, F32)],
        scratch_shapes=[pltpu.VMEM((tm, ff), BF16)],
        compiler_params=_cparams(2),
        name="ffn_swiglu",
    )(*args)
    return outs[-1] if proj is None else (outs[-1], *outs[:-1])


PROJ_T_ROWS = 512


def _proj_body(*refs, base, use_mod, kinds, kinds_t):
    it = iter(refs)
    x_ref, g_ref = next(it), next(it)
    mod_ref = next(it) if use_mod else None
    hb = _norm_mod(x_ref[...].astype(F32), g_ref[...], mod_ref, base).astype(BF16)
    _proj_emit(hb, it, kinds, kinds_t)


def _proj_emit(hb, it, kinds, kinds_t):
    w_ref = next(it) if kinds else None
    cs_ref = next(it) if any(k is not None for k in kinds) else None
    wt_ref = next(it) if kinds_t else None
    cst_ref = next(it) if _plan_ropes(kinds_t) else None
    o_ref = next(it) if kinds else None
    ot_ref = next(it) if kinds_t else None

    n_tiles = len(kinds)
    t = 0
    while t < n_tiles:
        w = min(MXU_WIDTH // LANES, n_tiles - t)
        y = _dot(hb, w_ref[:, t * LANES:(t + w) * LANES])
        for j in range(w):
            yj = y[:, j * LANES:(j + 1) * LANES]
            if kinds[t + j] is not None:
                yj = _rope_tile(yj, cs_ref[0], cs_ref[1])
            o_ref[:, (t + j) * LANES:(t + j + 1) * LANES] = yj.astype(o_ref.dtype)
        t += w

    half = HEAD_DIM // 2
    n_rows_t = len(kinds_t) * HEAD_DIM
    for r0 in range(0, n_rows_t, PROJ_T_ROWS):
        rows = min(PROJ_T_ROWS, n_rows_t - r0)
        yt = _dot_nt(wt_ref[r0:r0 + rows, :], hb)
        for gi in range(rows // HEAD_DIM):
            src = yt[gi * HEAD_DIM:(gi + 1) * HEAD_DIM]
            for dst, rope, scale in kinds_t[r0 // HEAD_DIM + gi]:
                tile = src
                if rope:
                    swapped = jnp.concatenate([src[half:], src[:half]], axis=0)
                    tile = src * cst_ref[0] + swapped * cst_ref[1]
                if scale != 1.0:
                    tile = tile * scale
                ot_ref[dst * HEAD_DIM:(dst + 1) * HEAD_DIM, :] = tile.astype(ot_ref.dtype)


def _plan_ropes(plan):
    return any(rope for outs in plan for _, rope, _ in outs)


def _plan(groups):
    return [((g, rope, scale),) for g, (rope, scale) in enumerate(groups)]


def _proj(x, in_width, in_block, g, mods_l, mod_row, base, w=None, kinds=(), cs=None, out_dtype=BF16,
          wt=None, kinds_t=(), cst=None):
    bx, n, _ = x.shape
    tm = min(512, n)
    use_mod = mods_l is not None
    kinds, kinds_t = tuple(kinds), tuple(kinds_t)
    in_specs = [
        pl.BlockSpec((None, tm, in_width), lambda b, i: (b, i, in_block)),
        pl.BlockSpec((1, in_width), lambda b, i: (0, 0)),
    ]
    args = [x, g.reshape(1, in_width)]
    if use_mod:
        in_specs.append(pl.BlockSpec((None, N_MOD, in_width), lambda b, i: (mod_row(b), 0, 0)))
        args.append(mods_l)
    p_specs, p_args, out_specs, out_shape = _proj_io(bx, n, tm, in_width, w, kinds, cs, out_dtype, wt, kinds_t, cst)
    outs = pl.pallas_call(
        functools.partial(_proj_body, base=base, use_mod=use_mod, kinds=kinds, kinds_t=kinds_t),
        grid=(bx, n // tm),
        in_specs=in_specs + p_specs,
        out_specs=out_specs,
        out_shape=out_shape,
        compiler_params=_cparams(2),
        name="norm_proj",
    )(*args, *p_args)
    return outs[0] if len(outs) == 1 else outs


def _proj_io(bx, n, tm, in_width, w, kinds, cs, out_dtype, wt, kinds_t, cst):
    in_specs, args, out_specs, out_shape = [], [], [], []
    if kinds:
        f = w.shape[1]
        assert f == len(kinds) * LANES
        in_specs.append(_resident((in_width, f), lambda b, i: (0, 0)))
        args.append(w)
        if any(k is not None for k in kinds):
            in_specs.append(pl.BlockSpec((2, tm, LANES), lambda b, i: (0, i, 0)))
            args.append(cs)
        out_specs.append(pl.BlockSpec((None, tm, f), lambda b, i: (b, i, 0)))
        out_shape.append(jax.ShapeDtypeStruct((bx, n, f), out_dtype))
    if kinds_t:
        fw = wt.shape[0]
        assert fw == len(kinds_t) * HEAD_DIM
        dsts = sorted(dst for outs in kinds_t for dst, _, _ in outs)
        assert dsts == list(range(len(dsts)))
        ft = len(dsts) * HEAD_DIM
        in_specs.append(_resident((fw, in_width), lambda b, i: (0, 0)))
        args.append(wt)
        if _plan_ropes(kinds_t):
            in_specs.append(pl.BlockSpec((2, HEAD_DIM, tm), lambda b, i: (0, 0, i)))
            args.append(cst)
        out_specs.append(pl.BlockSpec((None, ft, tm), lambda b, i: (b, 0, i)))
        out_shape.append(jax.ShapeDtypeStruct((bx, ft, n), BF16))
    return in_specs, args, out_specs, out_shape


def _lat_rows(blk):
    return lambda b, h, i: (b, 0, blk(h))


def _lat_t(blk):
    return lambda b, h, i: (b, blk(h), 0)


def _ctx_t(blk):
    return lambda b, h, i: (0, blk(h), b)


def _mla_body(*refs, tk, has_lat):
    if has_lat:
        qt_ref, kn_ref, kr_ref, vt_ref, knc_ref, krc_ref, vtc_ref, o_ref, s_a, s_b, kcat_ref = refs
    else:
        qt_ref, knc_ref, krc_ref, vtc_ref, o_ref, s_a, s_b = refs
        kcat_ref = vt_ref = None
    qt = qt_ref[...]
    kc = jnp.concatenate([knc_ref[...], krc_ref[...].astype(BF16)], axis=1)
    if has_lat:
        @pl.when(pl.program_id(2) == 0)
        def _():
            kcat_ref[:, :LANES] = kn_ref[...]
            kcat_ref[:, LANES:] = kr_ref[...].astype(BF16)

    st = _attend(qt, kc, vtc_ref[...], (s_a, s_b), qt, kcat_ref, vt_ref, tk)
    o_ref[...] = (st.acc / st.l).T.astype(o_ref.dtype)


def _mla_attn(qt, kvc, p1c, vtc, kv=None, p1=None, vt=None):
    h = MLA_HEADS
    has_lat = kv is not None
    bsz, c = kvc.shape[0], kvc.shape[1]
    lq = qt.shape[2] if has_lat else c
    tq = min(1024, lq)
    rope_tile = (MLA_Q_RANK + MLA_KV_RANK) // LANES
    head = lambda hh: hh
    in_specs, args, scratch, tk = [], [qt], [], 0
    if has_lat:
        lk = kv.shape[1]
        tk = min(512, lk)
        in_specs += [
            pl.BlockSpec((None, 2 * LANES, tq), lambda b, hh, i: (b, hh, i)),
            pl.BlockSpec((None, lk, LANES), _lat_rows(head)),
            pl.BlockSpec((None, lk, LANES), _lat_rows(lambda hh: rope_tile)),
            pl.BlockSpec((None, LANES, lk), _lat_t(head)),
        ]
        args += [kv, p1, vt]
        scratch = [pltpu.VMEM((lk, 2 * LANES), BF16)]
    else:
        in_specs.append(pl.BlockSpec((None, 2 * LANES, tq), _ctx_t(head)))
    in_specs += [
        pl.BlockSpec((None, c, LANES), _lat_rows(head)),
        pl.BlockSpec((None, c, LANES), _lat_rows(lambda hh: rope_tile)),
        pl.BlockSpec((None, LANES, c), _ctx_t(head)),
    ]
    args += [kvc, p1c, vtc]
    return pl.pallas_call(
        functools.partial(_mla_body, tk=tk, has_lat=has_lat),
        grid=(bsz, h, lq // tq),
        in_specs=in_specs,
        out_specs=pl.BlockSpec((None, tq, LANES), lambda b, hh, i: (b, i, hh)),
        out_shape=jax.ShapeDtypeStruct((bsz, lq, h * MLA_V), BF16),
        scratch_shapes=[pltpu.VMEM((max(tk, c), tq), F32)] * 2 + scratch,
        compiler_params=_cparams(3),
        name="mla_attention",
    )(*args)


def _diff_body(*refs, lam_init, tk, has_lat):
    if has_lat:
        qt_ref, k_ref, vt_ref, kc_ref, vtc_ref, lam_ref, ng_ref, o_ref, s_a, s_b = refs
        q_plain, q_rot = _stack_heads_t(qt_ref[LANES:, :]), _stack_heads_t(qt_ref[:LANES, :])
    else:
        qt_ref, kc_ref, vtc_ref, lam_ref, ng_ref, o_ref, s_a, s_b = refs
        q_plain, q_rot, k_ref, vt_ref = _stack_heads_t(qt_ref[...]), None, None, None
    tq = qt_ref.shape[1]
    st = _attend(q_plain, kc_ref[...], vtc_ref[...], (s_a, s_b), q_rot, k_ref, vt_ref, tk)
    lp = lam_ref[...]
    lam = (jnp.exp(jnp.sum(lp[0:1] * lp[1:2], axis=-1, keepdims=True))
           - jnp.exp(jnp.sum(lp[2:3] * lp[3:4], axis=-1, keepdims=True)) + lam_init)
    o = st.acc / st.l
    o = (o[:, :tq] - lam * o[:, tq:]).T
    o_ref[...] = (_rms(o) * ng_ref[...] * (1.0 - lam_init)).astype(o_ref.dtype)


def _diff_attn(kc, ptc, lam_params, norm_g, lam_init, qtc_blk, vtc_blk, kx=None, ptx=None):
    h = DIFF_HEADS
    has_lat = kx is not None
    bsz, c = kc.shape[0], kc.shape[1]
    lq = kx.shape[1] if has_lat else c
    tq = min(512, lq)
    head = lambda hh: hh
    in_specs, args, tk = [], [], 0
    if has_lat:
        lk = kx.shape[1]
        tk = min(512, lk)
        in_specs += [
            pl.BlockSpec((None, 2 * LANES, tq), lambda b, hh, i: (b, hh, i)),
            pl.BlockSpec((None, lk, LANES), _lat_rows(head)),
            pl.BlockSpec((None, LANES, lk), _lat_t(lambda hh: 2 * h + hh)),
        ]
        args += [ptx, kx, ptx]
    else:
        in_specs.append(pl.BlockSpec((None, LANES, tq), _ctx_t(lambda hh: qtc_blk + hh)))
        args.append(ptc)
    in_specs += [
        pl.BlockSpec((None, c, LANES), _lat_rows(head)),
        pl.BlockSpec((None, LANES, c), _ctx_t(lambda hh: vtc_blk + hh)),
        pl.BlockSpec((4, HEAD_DIM), lambda b, hh, i: (0, 0)),
        pl.BlockSpec((1, 2 * HEAD_DIM), lambda b, hh, i: (0, 0)),
    ]
    args += [kc, ptc, lam_params, norm_g.reshape(1, 2 * HEAD_DIM)]
    return pl.pallas_call(
        functools.partial(_diff_body, lam_init=lam_init, tk=tk, has_lat=has_lat),
        grid=(bsz, h, lq // tq),
        in_specs=in_specs,
        out_specs=pl.BlockSpec((None, tq, LANES), lambda b, hh, i: (b, i, hh)),
        out_shape=jax.ShapeDtypeStruct((bsz, lq, h * 2 * HEAD_DIM), BF16),
        scratch_shapes=[pltpu.VMEM((max(tk, c), 2 * tq), F32)] * 2,
        compiler_params=_cparams(3),
        name="diff_attention",
    )(*args)


def _swa_body(*refs, tq, has_lat):
    if has_lat:
        sink_ref, qr_ref, qp_ref, k_ref, vt_ref, kc_ref, vtc_ref, o_ref, s_a, s_b = refs
    else:
        sink_ref, qp_ref, kc_ref, vtc_ref, o_ref, s_a, s_b = refs
    bufs = (s_a, s_b)
    pair = pl.program_id(1)
    kc, vtc = kc_ref[...], vtc_ref[...]
    c = kc.shape[0]
    span = tq + 2 * SWA_WINDOW if has_lat else 0
    first_half = lax.broadcasted_iota(jnp.int32, (1, 2 * tq), 1) < tq
    if has_lat:
        lk = k_ref.shape[0]
        q0 = pl.program_id(2) * tq
        start = pl.multiple_of(jnp.clip(q0 - SWA_WINDOW, 0, lk - span), SWA_WINDOW)
        keys = pl.ds(start, span)
        kpos = start + lax.broadcasted_iota(jnp.int32, (span, 2 * tq), 0)
        qpos = q0 + (lax.broadcasted_iota(jnp.int32, (span, 2 * tq), 1) & (tq - 1))
        valid = jnp.abs(kpos - qpos) <= SWA_WINDOW
    low = _row_is_low((LANES, tq))

    def scores(g):
        rows = slice(g * LANES, (g + 1) * LANES)
        buf = bufs[g % 2]
        buf[0:c, :] = _dot(kc, _stack_heads_t(qp_ref[rows, :]))
        if has_lat:
            s_w = _dot(k_ref[keys, :], _stack_heads_t(qr_ref[rows, :]))
            buf[c:c + span, :] = jnp.where(valid, s_w, NEG_INF)

    scores(0)
    for g in range(SWA_GROUP):
        if g + 1 < SWA_GROUP:
            scores(g + 1)
        sink = jnp.where(first_half, sink_ref[2 * pair * SWA_GROUP + g],
                         sink_ref[(2 * pair + 1) * SWA_GROUP + g]) * LOG2E
        s = bufs[g % 2][...]
        m = jnp.maximum(jnp.max(s, axis=0, keepdims=True), sink)
        p = jnp.exp2(s - m)
        l = jnp.sum(p, axis=0, keepdims=True) + jnp.exp2(sink - m)
        p = p.astype(BF16)
        o = _dot(vtc, p[0:c])
        if has_lat:
            o = o + _dot(vt_ref[:, keys], p[c:])
        o = o / l
        o_ref[:, g * LANES:(g + 1) * LANES] = jnp.where(low, o[:, :tq], o[:, tq:]).T.astype(o_ref.dtype)


def _swa_attn(sink, kc, ptc, kx=None, ptx=None):
    n_pairs = SWA_KV_HEADS // 2
    qw = SWA_GROUP * LANES
    has_lat = kx is not None
    bsz, c = kc.shape[0], kc.shape[1]
    lq = kx.shape[1] if has_lat else c
    tq = min(256, lq)
    pair = lambda p: p
    in_specs = [pl.BlockSpec(memory_space=pltpu.SMEM)]
    args = [sink]
    span = 0
    if has_lat:
        lk = kx.shape[1]
        span = tq + 2 * SWA_WINDOW
        in_specs += [
            pl.BlockSpec((None, qw, tq), lambda b, p, i: (b, p, i)),
            pl.BlockSpec((None, qw, tq), lambda b, p, i: (b, n_pairs + p, i)),
            pl.BlockSpec((None, lk, LANES), _lat_rows(pair)),
            pl.BlockSpec((None, LANES, lk), _lat_t(lambda p: 2 * n_pairs * SWA_GROUP + p)),
        ]
        args += [ptx, ptx, kx, ptx]
    else:
        in_specs.append(pl.BlockSpec((None, qw, tq), _ctx_t(pair)))
        args.append(ptc)
    in_specs += [
        pl.BlockSpec((None, c, LANES), _lat_rows(pair)),
        pl.BlockSpec((None, LANES, c), _ctx_t(lambda p: n_pairs * SWA_GROUP + p)),
    ]
    args += [kc, ptc]
    return pl.pallas_call(
        functools.partial(_swa_body, tq=tq, has_lat=has_lat),
        grid=(bsz, n_pairs, lq // tq),
        in_specs=in_specs,
        out_specs=pl.BlockSpec((None, tq, qw), lambda b, p, i: (b, i, p)),
        out_shape=jax.ShapeDtypeStruct((bsz, lq, n_pairs * qw), BF16),
        scratch_shapes=[pltpu.VMEM((c + span, 2 * tq), F32)] * 2,
        compiler_params=_cparams(3),
        name="swa_attention",
    )(*args)


def _na_body(*refs, n_rows, has_lat):
    if has_lat:
        qt_ref, k_ref, vt_ref, bias_ref, kc_ref, vtc_ref, o_ref, s_a, s_b = refs
    else:
        qt_ref, kc_ref, vtc_ref, o_ref, s_a, s_b = refs
    bufs = (s_a, s_b)
    kc, vtc = kc_ref[...], vtc_ref[...]
    c = kc.shape[0]
    n_steps = qt_ref.shape[1] // LANES
    win = NA_WIN_ROWS * GRID_W
    low = _row_is_low((LANES, LANES))

    def scores(u):
        q = _stack_heads_t(qt_ref[:, u * LANES:(u + 1) * LANES])
        buf = bufs[u % 2]
        buf[0:c, :] = _dot(kc, q)
        if not has_lat:
            return None
        r = (pl.program_id(2) * n_steps + u) * NA_STEP_ROWS
        ws = jnp.clip(r - NA_KH // 2, 0, n_rows - NA_WIN_ROWS)
        cls = lax.shift_right_logical(r - ws, 1)
        keys = pl.ds(pl.multiple_of(ws * GRID_W, LANES), win)
        buf[c:c + win, :] = _dot(k_ref[keys, :], q) + bias_ref[cls]
        return keys

    keys = scores(0)
    for u in range(n_steps):
        next_keys = scores(u + 1) if u + 1 < n_steps else None
        s = bufs[u % 2][...]
        p = jnp.exp2(s - jnp.max(s, axis=0, keepdims=True))
        l = jnp.sum(p, axis=0, keepdims=True)
        p = p.astype(BF16)
        o = _dot(vtc, p[0:c])
        if has_lat:
            o = o + _dot(vt_ref[:, keys], p[c:])
        o = o / l
        o_ref[u * LANES:(u + 1) * LANES, :] = jnp.where(low, o[:, :LANES], o[:, LANES:]).T.astype(o_ref.dtype)
        keys = next_keys


def _na_attn(kc, ptc, bias=None, kx=None, ptx=None):
    n_pairs = NA_HEADS // 2
    has_lat = kx is not None
    bsz, c = kc.shape[0], kc.shape[1]
    pair = lambda p: p
    in_specs, args, n_rows = [], [], 0
    if has_lat:
        lq = kx.shape[1]
        n_rows = lq // GRID_W
        tq = min(1024, lq)
        in_specs += [
            pl.BlockSpec((None, LANES, tq), lambda b, p, i: (b, p, i)),
            pl.BlockSpec((None, lq, LANES), _lat_rows(pair)),
            pl.BlockSpec((None, LANES, lq), _lat_t(lambda p: n_pairs + p)),
            pl.BlockSpec((None, NA_CLASSES, NA_WIN_ROWS * GRID_W, 2 * LANES), lambda b, p, i: (p, 0, 0, 0)),
        ]
        args += [ptx, kx, ptx, bias]
    else:
        lq = tq = c
        in_specs.append(pl.BlockSpec((None, LANES, tq), _ctx_t(pair)))
        args.append(ptc)
    in_specs += [
        pl.BlockSpec((None, c, LANES), _lat_rows(pair)),
        pl.BlockSpec((None, LANES, c), _ctx_t(lambda p: n_pairs + p)),
    ]
    args += [kc, ptc]
    return pl.pallas_call(
        functools.partial(_na_body, n_rows=n_rows, has_lat=has_lat),
        grid=(bsz, n_pairs, lq // tq),
        in_specs=in_specs,
        out_specs=pl.BlockSpec((None, tq, LANES), lambda b, p, i: (b, i, p)),
        out_shape=jax.ShapeDtypeStruct((bsz, lq, NA_HEADS * HEAD_DIM), BF16),
        scratch_shapes=[pltpu.VMEM((c + (NA_WIN_ROWS * GRID_W if has_lat else 0), 2 * LANES), F32)] * 2,
        compiler_params=_cparams(3),
        name="na_attention",
    )(*args)


def _rope_tables(seq):
    t = jnp.arange(seq, dtype=jnp.int32)
    row = (t // GRID_W).astype(F32)
    col = (t % GRID_W).astype(F32)
    n_freq = ROPE_DIM // 4
    inv = jnp.exp(-math.log(ROPE_BASE) * jnp.arange(n_freq, dtype=F32) / n_freq)
    ang = jnp.concatenate([row[:, None] * inv, col[:, None] * inv], axis=-1)
    cos, sin = jnp.cos(ang), jnp.sin(ang)
    c_head = jnp.concatenate([cos, cos], axis=-1)
    s_head = jnp.concatenate([-sin, sin], axis=-1)
    cs = jnp.stack([jnp.concatenate([c_head, c_head], axis=-1), jnp.concatenate([s_head, s_head], axis=-1)])
    cst = jnp.stack([c_head.T, s_head.T])
    return cs, cst


def _na_bias_table(rpb, n_rows):
    assert n_rows % 2 == 0 and n_rows >= NA_WIN_ROWS + 2 and NA_STEP_ROWS == 2
    rel_start = np.array([[0, 0], [0, 0], [0, 1], [2, 2], [2, 2]])
    dq = np.array([[0, 1], [2, 3], [4, 5], [6, 7], [8, 9]])
    jr = np.arange(NA_WIN_ROWS)
    row_ok = (jr >= rel_start[:, :, None]) & (jr < rel_start[:, :, None] + NA_KH)
    ridx = np.clip(jr - dq[:, :, None] + (NA_KH - 1), 0, 2 * NA_KH - 2)
    n_pairs = rpb.shape[0] // 2
    rows = jnp.where(row_ok[None, :, :, :, None], rpb.astype(F32)[:, ridx] * LOG2E, NEG_INF)
    pad = GRID_W - NA_KW
    rows = jnp.pad(rows[..., ::-1], [(0, 0)] * 4 + [(pad, pad + 1)])
    rows = rows.reshape(n_pairs, 2, NA_CLASSES, NA_STEP_ROWS, NA_WIN_ROWS, LANES)
    rows = rows.transpose(0, 2, 1, 3, 4, 5).reshape(n_pairs, NA_CLASSES, -1, LANES)
    return pl.pallas_call(
        _na_bias_body,
        grid=(n_pairs, NA_CLASSES),
        in_specs=[pl.BlockSpec((None, None, rows.shape[2], LANES), lambda p, c: (p, c, 0, 0))],
        out_specs=pl.BlockSpec((None, None, NA_WIN_ROWS * GRID_W, 2 * LANES), lambda p, c: (p, c, 0, 0)),
        out_shape=jax.ShapeDtypeStruct((n_pairs, NA_CLASSES, NA_WIN_ROWS * GRID_W, 2 * LANES), F32),
        compiler_params=_cparams(2),
        name="na_bias_table",
    )(rows)


def _na_bias_body(r_ref, o_ref):
    kcol = lax.broadcasted_iota(jnp.int32, (GRID_W, LANES), 0)
    lane = lax.broadcasted_iota(jnp.int32, (GRID_W, LANES), 1)
    col = lane & (GRID_W - 1)
    col_start = jnp.clip(col - NA_KW // 2, 0, GRID_W - NA_KW)
    in_win = (kcol >= col_start) & (kcol < col_start + NA_KW)
    low = lane < GRID_W
    for e in range(2):
        for jr in range(NA_WIN_ROWS):
            halves = []
            for rr in range(NA_STEP_ROWS):
                t = (e * NA_STEP_ROWS + rr) * NA_WIN_ROWS + jr
                v = jnp.broadcast_to(r_ref[t:t + 1, :], (GRID_W, LANES))
                shift = LANES - (GRID_W - 1) if rr == 0 else LANES - (2 * GRID_W - 1)
                halves.append(pltpu.roll(v, shift, 1, stride=1, stride_axis=0))
            tile = jnp.where(in_win, jnp.where(low, halves[0], halves[1]), NEG_INF)
            o_ref[jr * GRID_W:(jr + 1) * GRID_W, e * LANES:(e + 1) * LANES] = tile


def _swa_q_perm():
    idx = []
    for pair in range(SWA_KV_HEADS // 2):
        for g in range(SWA_GROUP):
            for e in range(2):
                head = (2 * pair + e) * SWA_GROUP + g
                idx.extend(range(head * HEAD_DIM, (head + 1) * HEAD_DIM))
    return np.asarray(idx)


def kernel(x, c, ctx, c_ctx, mod_w, mod_b, norm_g, final_norm_g, ffn_w_in, ffn_w_out,
           mla_w_down, mla_q_norm, mla_kv_norm, mla_w_uq, mla_w_ukv, mla_w_o,
           swa_w_qkv, swa_sink, swa_w_o, na_w_qkv, na_rpb, na_w_o,
           diff_w_qkv, diff_lambda, diff_norm_g, diff_w_o):
    bsz, seq, d = x.shape
    n_ctx = ctx.shape[1]
    depth = mod_w.shape[0]
    assert bsz + 1 <= MOD_ROWS and seq % GRID_W == 0 and n_ctx % LANES == 0

    cond = jnp.zeros((MOD_ROWS, d), F32).at[:bsz].set(c).at[bsz].set(c_ctx)
    mods = _mod_call(cond, mod_w, mod_b).reshape(depth, MOD_ROWS, N_MOD, d)
    cs, cst = _rope_tables(seq)
    w_in, w_out = ffn_w_in.astype(BF16), ffn_w_out.astype(BF16)
    xc = ctx.reshape(1, bsz * n_ctx, d)
    x_row = lambda b: b
    c_row = lambda b: bsz
    c64 = HEAD_DIM ** -0.5 * LOG2E

    def per_batch(t):
        return t.reshape(bsz, n_ctx, t.shape[-1])

    for layer in range(depth):
        kind = layer % 4
        j = layer // 4
        last = layer == depth - 1
        ml = mods[layer]

        def ffn1(t, row, **proj):
            return _ffn(t, norm_g[layer, 0], ml, row, w_in, w_out, layer, 0, 0,
                        proj=dict(g=norm_g[layer, 1], base=3, **proj))

        if kind == 0:
            h = MLA_HEADS
            qk = MLA_Q_RANK + MLA_KV_RANK
            c_mla = (MLA_NOPE + ROPE_DIM) ** -0.5 * LOG2E
            wd = mla_w_down[j].astype(BF16)
            z = jnp.zeros((d, ROPE_DIM), BF16)
            w1x = jnp.concatenate([wd, z], axis=1)
            w1c = jnp.concatenate([wd[:, :qk], z, wd[:, qk:]], axis=1)
            n1 = w1x.shape[1] // LANES
            x, p1x = ffn1(x, x_row, w=w1x, kinds=[None] * (n1 - 1) + [True], cs=cs, out_dtype=F32)
            xc, p1c = ffn1(xc, c_row, w=w1c, kinds=[None] * n1, out_dtype=F32)
            wqt = mla_w_uq[j].astype(BF16).T
            wkv = mla_w_ukv[j].astype(BF16).reshape(MLA_KV_RANK, h, MLA_NOPE + MLA_V)
            wk = wkv[:, :, :MLA_NOPE].reshape(MLA_KV_RANK, -1)
            wvt = wkv[:, :, MLA_NOPE:].reshape(MLA_KV_RANK, -1).T
            kv_blk = MLA_Q_RANK // MLA_KV_RANK
            q_kinds, q_plain = [], []
            for hh in range(h):
                o4 = 4 * hh
                nope = [((o4, False, c_mla),), ((o4 + 1, False, c_mla),)]
                q_kinds += nope + [((o4 + 2, True, c_mla), (o4 + 3, False, c_mla))]
                q_plain += nope + [((o4 + 2, False, c_mla), (o4 + 3, False, c_mla))]
            v_kinds = _plan([(False, 1.0)] * (2 * h))
            k_none = [None] * h
            qtx = _proj(p1x, MLA_Q_RANK, 0, mla_q_norm[j], None, None, 0, wt=wqt, kinds_t=q_kinds, cst=cst)
            kvx, vtx = _proj(p1x, MLA_KV_RANK, kv_blk, mla_kv_norm[j], None, None, 0, w=wk, kinds=k_none,
                             wt=wvt, kinds_t=v_kinds)
            kvc, vtc = _proj(p1c, MLA_KV_RANK, kv_blk, mla_kv_norm[j], None, None, 0, w=wk, kinds=k_none,
                             wt=wvt, kinds_t=v_kinds)
            kvc, p1c_b = per_batch(kvc), per_batch(p1c)
            ox = _mla_attn(qtx, kvc, p1c_b, vtc, kvx, p1x, vtx)
            w_o = mla_w_o[j].astype(BF16)
            if not last:
                qtc = _proj(p1c, MLA_Q_RANK, 0, mla_q_norm[j], None, None, 0, wt=wqt, kinds_t=q_plain)
                oc = _mla_attn(qtc, kvc, p1c_b, vtc)
        elif kind == 1:
            wqkv = swa_w_qkv[j].astype(BF16)
            nq = SWA_KV_HEADS * SWA_GROUP * HEAD_DIM
            nk = SWA_KV_HEADS * HEAD_DIM
            perm = _swa_q_perm()
            wqt = wqkv[:, :nq][:, perm].T
            wk, wvt = wqkv[:, nq:nq + nk], wqkv[:, nq + nk:].T
            gq, gv = nq // HEAD_DIM, nk // HEAD_DIM
            wt = jnp.concatenate([wqt, wvt], axis=0)
            ktx = ([((g, True, c64), (gq + g, False, c64)) for g in range(gq)]
                   + [((2 * gq + g, False, 1.0),) for g in range(gv)])
            x, kx, ptx = ffn1(x, x_row, w=wk, kinds=[True] * (nk // LANES), cs=cs, wt=wt, kinds_t=ktx, cst=cst)
            xc, kc, ptc = ffn1(xc, c_row, w=wk, kinds=[None] * (nk // LANES), wt=wt,
                               kinds_t=_plan([(False, c64)] * gq + [(False, 1.0)] * gv))
            kc = per_batch(kc)
            ox = _swa_attn(swa_sink[j], kc, ptc, kx, ptx)
            w_o = swa_w_o[j].astype(BF16)[perm, :]
            if not last:
                oc = _swa_attn(swa_sink[j], kc, ptc)
        elif kind == 2:
            wqkv = na_w_qkv[j].astype(BF16)
            nh = NA_HEADS * HEAD_DIM
            wk = wqkv[:, nh:2 * nh]
            wt = jnp.concatenate([wqkv[:, :nh], wqkv[:, 2 * nh:]], axis=1).T
            kt = _plan([(False, c64)] * NA_HEADS + [(False, 1.0)] * NA_HEADS)
            k_none = [None] * (nh // LANES)
            x, kx, ptx = ffn1(x, x_row, w=wk, kinds=k_none, wt=wt, kinds_t=kt)
            xc, kc, ptc = ffn1(xc, c_row, w=wk, kinds=k_none, wt=wt, kinds_t=kt)
            kc = per_batch(kc)
            ox = _na_attn(kc, ptc, _na_bias_table(na_rpb[j], seq // GRID_W), kx, ptx)
            w_o = na_w_o[j].astype(BF16)
            if not last:
                oc = _na_attn(kc, ptc)
        else:
            h = DIFF_HEADS
            lam_init = 0.8 - 0.6 * math.exp(-0.3 * layer)
            wqkv = diff_w_qkv[j].astype(BF16)
            nq = 2 * h * HEAD_DIM
            wk = wqkv[:, nq:2 * nq]
            wqt = wqkv[:, :nq].T
            wvt = wqkv[:, 2 * nq:].T
            wt = jnp.concatenate([wqt, wvt], axis=0)
            ktx = ([((4 * (g // 2) + g % 2, True, c64), (4 * (g // 2) + 2 + g % 2, False, c64))
                    for g in range(2 * h)] + [((4 * h + g, False, 1.0),) for g in range(2 * h)])
            ktc = _plan([(False, c64)] * (2 * h) + [(False, 1.0)] * (2 * h))
            x, kx, ptx = ffn1(x, x_row, w=wk, kinds=[True] * h, cs=cs, wt=wt, kinds_t=ktx, cst=cst)
            xc, kc, ptc = ffn1(xc, c_row, w=wk, kinds=[None] * h, wt=wt, kinds_t=ktc)
            kc = per_batch(kc)
            ox = _diff_attn(kc, ptc, diff_lambda[j], diff_norm_g[j], lam_init, 0, h, kx, ptx)
            w_o = diff_w_o[j].astype(BF16)
            if not last:
                oc = _diff_attn(kc, ptc, diff_lambda[j], diff_norm_g[j], lam_init, 0, h)

        x = _ffn(x, norm_g[layer, 2], ml, x_row, w_in, w_out, layer, 1, 6,
                 final_g=final_norm_g if last else None, mixer=(ox, w_o, 5))
        if not last:
            oc = oc.reshape(1, bsz * n_ctx, oc.shape[-1])
            xc = _ffn(xc, norm_g[layer, 2], ml, c_row, w_in, w_out, layer, 1, 6, mixer=(oc, w_o, 5))
    return x
```

```python
import functools
import math

import jax
import jax.numpy as jnp
import numpy as np
from jax import lax
from jax.experimental import pallas as pl
from jax.experimental.pallas import tpu as pltpu

F32 = jnp.float32
BF16 = jnp.bfloat16

GRID_W = 64
N_MOD = 9
EPS = 1e-6
ROPE_BASE = 10000.0
ROPE_DIM = 64
NEG_INF = -1e30
LOG2E = math.log2(math.e)

MLA_HEADS = 8
MLA_Q_RANK = 512
MLA_KV_RANK = 256
MLA_NOPE = 128
MLA_V = 128

SWA_KV_HEADS = 4
SWA_GROUP = 4
SWA_WINDOW = 128

NA_HEADS = 16
NA_KH = 8
NA_KW = 16
NA_STEP_ROWS = 2
NA_WIN_ROWS = NA_KH + NA_STEP_ROWS
NA_CLASSES = 5

DIFF_HEADS = 8
HEAD_DIM = 64

LANES = 128
MXU_WIDTH = 256
VMEM_LIMIT_BYTES = 56 * 1024 * 1024
MOD_ROWS = 8


def _cparams(n_axes):
    return pltpu.CompilerParams(
        dimension_semantics=("arbitrary",) * n_axes,
        vmem_limit_bytes=VMEM_LIMIT_BYTES,
    )


def _resident(shape, index_map):
    return pl.BlockSpec(shape, index_map, pipeline_mode=pl.Buffered(1))


def _rms(x):
    return x * lax.rsqrt(jnp.mean(x * x, axis=-1, keepdims=True) + EPS)


def _norm_mod(x, g, mod_ref, base):
    y = _rms(x) * g
    if mod_ref is not None:
        y = y * (1.0 + mod_ref[base + 1:base + 2, :]) + mod_ref[base:base + 1, :]
    return y


def _dot(a, b):
    return jnp.dot(a, b, preferred_element_type=F32)


def _dot_nt(a, b):
    return lax.dot_general(a, b, (((1,), (1,)), ((), ())), preferred_element_type=F32)


def _row_is_low(shape):
    return lax.broadcasted_iota(jnp.int32, shape, 0) < HEAD_DIM


def _stack_heads_t(qt):
    low = _row_is_low(qt.shape)
    zero = jnp.zeros_like(qt)
    return jnp.concatenate([jnp.where(low, qt, zero), jnp.where(low, zero, qt)], axis=1)


def _rope_tile(y, c, s):
    lane = lax.broadcasted_iota(jnp.int32, y.shape, 1)
    first = (lane & (HEAD_DIM - 1)) < HEAD_DIM // 2
    swapped = jnp.where(first, pltpu.roll(y, LANES - HEAD_DIM // 2, 1), pltpu.roll(y, HEAD_DIM // 2, 1))
    return y * c + swapped * s


class _SoftmaxT:
    def __init__(self, m, l, acc):
        self.m, self.l, self.acc = m, l, acc

    @staticmethod
    def first(s, vt):
        m = jnp.max(s, axis=0, keepdims=True)
        p = jnp.exp2(s - m)
        return _SoftmaxT(m, jnp.sum(p, axis=0, keepdims=True), _dot(vt, p.astype(vt.dtype)))

    def update(self, s, vt):
        m_new = jnp.maximum(self.m, jnp.max(s, axis=0, keepdims=True))
        a = jnp.exp2(self.m - m_new)
        p = jnp.exp2(s - m_new)
        return _SoftmaxT(m_new, a * self.l + jnp.sum(p, axis=0, keepdims=True),
                         a * self.acc + _dot(vt, p.astype(vt.dtype)))


def _attend(qt_ctx, kc, vtc, bufs, qt_lat=None, k_ref=None, vt_ref=None, tk=0):
    c = kc.shape[0]
    n = 1 + (k_ref.shape[0] // tk if k_ref is not None else 0)
    width = qt_ctx.shape[1]
    n_split = 2 if width % (2 * MXU_WIDTH) == 0 else 1
    cols = [slice(u * width // n_split, (u + 1) * width // n_split) for u in range(n_split)]

    def write_scores(j, u):
        if j == 0:
            bufs[0][0:c, cols[u]] = _dot(kc, qt_ctx[:, cols[u]])
        else:
            bufs[j % 2][0:tk, cols[u]] = _dot(k_ref[(j - 1) * tk:j * tk, :], qt_lat[:, cols[u]])

    for u in range(n_split):
        write_scores(0, u)
    sts = [None] * n_split
    for j in range(n):
        for u in range(n_split):
            if j + 1 < n:
                write_scores(j + 1, u)
            if j == 0:
                sts[u] = _SoftmaxT.first(bufs[0][0:c, cols[u]], vtc)
            else:
                sts[u] = sts[u].update(bufs[j % 2][0:tk, cols[u]], vt_ref[:, (j - 1) * tk:j * tk])
    if n_split == 1:
        return sts[0]
    return _SoftmaxT(*(jnp.concatenate([getattr(st, f) for st in sts], axis=1) for f in ("m", "l", "acc")))


def _mod_body(c_ref, w_ref, b_ref, o_ref):
    c = c_ref[...]
    h = (c * jax.nn.sigmoid(c)).astype(BF16)
    o_ref[...] = _dot(h, w_ref[...].astype(BF16)) + b_ref[...]


def _mod_call(cond, mod_w, mod_b):
    depth, d, n = mod_w.shape
    tn = 1024
    return pl.pallas_call(
        _mod_body,
        grid=(depth, n // tn),
        in_specs=[
            pl.BlockSpec((MOD_ROWS, d), lambda l, j: (0, 0)),
            pl.BlockSpec((None, d, tn), lambda l, j: (l, 0, j)),
            pl.BlockSpec((None, 1, tn), lambda l, j: (l, 0, j)),
        ],
        out_specs=pl.BlockSpec((None, MOD_ROWS, tn), lambda l, j: (l, 0, j)),
        out_shape=jax.ShapeDtypeStruct((depth, MOD_ROWS, n), F32),
        compiler_params=_cparams(2),
        name="adaln_mod",
    )(cond, mod_w, mod_b.reshape(depth, 1, n))


def _ffn_body(x_ref, g_ref, mod_ref, win_hbm, wout_hbm, *rest, base, ff, chunk, final, mixer_base, proj,
              layer, half):
    it = iter(rest[:-5])
    o_ref, act_ref, win_ref, wout_ref, sem = rest[-5:]
    a_ref, wo_ref = (next(it), next(it)) if mixer_base is not None else (None, None)
    fg_ref = next(it) if final else None
    n_chunks = ff // chunk

    def copy(which, c):
        if which == 2:
            rows = pl.ds(c * chunk, chunk)
            return pltpu.make_async_copy(wout_hbm.at[layer, half, rows, :], wout_ref.at[rows, :], sem.at[2, c])
        cols = pl.ds(which * ff + c * chunk, chunk)
        return pltpu.make_async_copy(win_hbm.at[layer, half, :, cols], win_ref.at[:, cols], sem.at[which, c])

    g2_ref = next(it) if proj is not None else None
    proj_refs = list(it)

    def step(wait_weights):
        x = x_ref[...]
        if mixer_base is not None:
            x = x + mod_ref[mixer_base:mixer_base + 1, :] * _dot(a_ref[...], wo_ref[...])
        hb = _norm_mod(x, g_ref[...], mod_ref, base).astype(BF16)
        for c in range(n_chunks):
            c0 = c * chunk
            if wait_weights:
                copy(0, c).wait()
                copy(1, c).wait()
            g = _dot(hb, win_ref[:, c0:c0 + chunk])
            u = _dot(hb, win_ref[:, ff + c0:ff + c0 + chunk])
            act_ref[:, c0:c0 + chunk] = (g * jax.nn.sigmoid(g) * u).astype(BF16)
        if wait_weights:
            for c in range(n_chunks):
                copy(2, c).wait()
        y = x + (0.5 * mod_ref[base + 2:base + 3, :]) * _dot(act_ref[...], wout_ref[...])
        if final:
            y = _rms(y) * fg_ref[...]
        o_ref[...] = y
        if proj is not None:
            proj_base, kinds, kinds_t = proj
            _proj_emit(_norm_mod(y, g2_ref[...], mod_ref, proj_base).astype(BF16), iter(proj_refs), kinds, kinds_t)

    first = jnp.logical_and(pl.program_id(0) == 0, pl.program_id(1) == 0)

    @pl.when(first)
    def _():
        for c in range(n_chunks):
            copy(0, c).start()
            copy(1, c).start()
        for c in range(n_chunks):
            copy(2, c).start()
        step(True)

    @pl.when(jnp.logical_not(first))
    def _():
        step(False)


def _ffn(x, g, mods_l, mod_row, w_in, w_out, layer, half, base, final_g=None, mixer=None, proj=None):
    bx, n, d = x.shape
    ff = w_out.shape[2]
    tm = min(512, n)
    final = final_g is not None
    in_specs = [
        pl.BlockSpec((None, tm, d), lambda b, i: (b, i, 0)),
        pl.BlockSpec((1, d), lambda b, i: (0, 0)),
        pl.BlockSpec((None, N_MOD, d), lambda b, i: (mod_row(b), 0, 0)),
        pl.BlockSpec(memory_space=pl.ANY),
        pl.BlockSpec(memory_space=pl.ANY),
    ]
    args = [x, g.reshape(1, d), mods_l, w_in, w_out]
    mixer_base = None
    if mixer is not None:
        a, w_o, mixer_base = mixer
        k = a.shape[-1]
        in_specs += [pl.BlockSpec((None, tm, k), lambda b, i: (b, i, 0)), _resident((k, d), lambda b, i: (0, 0))]
        args += [a, w_o]
    if final:
        in_specs.append(pl.BlockSpec((1, d), lambda b, i: (0, 0)))
        args.append(final_g.reshape(1, d))
    out_specs, out_shape, proj_static = [], [], None
    if proj is not None:
        kinds, kinds_t = tuple(proj.get("kinds", ())), tuple(proj.get("kinds_t", ()))
        p_specs, p_args, out_specs, out_shape = _proj_io(
            bx, n, tm, d, proj.get("w"), kinds, proj.get("cs"), proj.get("out_dtype", BF16),
            proj.get("wt"), kinds_t, proj.get("cst"))
        in_specs += [pl.BlockSpec((1, d), lambda b, i: (0, 0))] + p_specs
        args += [proj["g"].reshape(1, d)] + p_args
        proj_static = (proj["base"], kinds, kinds_t)
    outs = pl.pallas_call(
        functools.partial(_ffn_body, base=base, ff=ff, chunk=MXU_WIDTH, final=final, mixer_base=mixer_base,
                          proj=proj_static, layer=layer, half=half),
        grid=(bx, n // tm),
        in_specs=in_specs,
        out_specs=out_specs + [pl.BlockSpec((None, tm, d), lambda b, i: (b, i, 0))],
        out_shape=out_shape + [jax.ShapeDtypeStruct(x.shape, F32)],
        scratch_shapes=[pltpu.VMEM((tm, ff), BF16), pltpu.VMEM((d, 2 * ff), BF16), pltpu.VMEM((ff, d), BF16),
                        pltpu.SemaphoreType.DMA((3, ff // MXU_WIDTH))],
        compiler_params=_cparams(2),
        name="ffn_swiglu",
    )(*args)
    return outs[-1] if proj is None else (outs[-1], *outs[:-1])


PROJ_T_ROWS = 512


def _proj_body(*refs, base, use_mod, kinds, kinds_t):
    it = iter(refs)
    x_ref, g_ref = next(it), next(it)
    mod_ref = next(it) if use_mod else None
    hb = _norm_mod(x_ref[...].astype(F32), g_ref[...], mod_ref, base).astype(BF16)
    _proj_emit(hb, it, kinds, kinds_t)


def _proj_emit(hb, it, kinds, kinds_t):
    w_ref = next(it) if kinds else None
    cs_ref = next(it) if any(k is not None for k in kinds) else None
    wt_ref = next(it) if kinds_t else None
    cst_ref = next(it) if _plan_ropes(kinds_t) else None
    o_ref = next(it) if kinds else None
    ot_ref = next(it) if kinds_t else None

    n_tiles = len(kinds)
    t = 0
    while t < n_tiles:
        w = min(MXU_WIDTH // LANES, n_tiles - t)
        y = _dot(hb, w_ref[:, t * LANES:(t + w) * LANES])
        for j in range(w):
            yj = y[:, j * LANES:(j + 1) * LANES]
            if kinds[t + j] is not None:
                yj = _rope_tile(yj, cs_ref[0], cs_ref[1])
            o_ref[:, (t + j) * LANES:(t + j + 1) * LANES] = yj.astype(o_ref.dtype)
        t += w

    half = HEAD_DIM // 2
    n_rows_t = len(kinds_t) * HEAD_DIM
    for r0 in range(0, n_rows_t, PROJ_T_ROWS):
        rows = min(PROJ_T_ROWS, n_rows_t - r0)
        yt = _dot_nt(wt_ref[r0:r0 + rows, :], hb)
        for gi in range(rows // HEAD_DIM):
            src = yt[gi * HEAD_DIM:(gi + 1) * HEAD_DIM]
            for dst, rope, scale in kinds_t[r0 // HEAD_DIM + gi]:
                tile = src
                if rope:
                    swapped = jnp.concatenate([src[half:], src[:half]], axis=0)
                    tile = src * cst_ref[0] + swapped * cst_ref[1]
                if scale != 1.0:
                    tile = tile * scale
                ot_ref[dst * HEAD_DIM:(dst + 1) * HEAD_DIM, :] = tile.astype(ot_ref.dtype)


def _plan_ropes(plan):
    return any(rope for outs in plan for _, rope, _ in outs)


def _plan(groups):
    return [((g, rope, scale),) for g, (rope, scale) in enumerate(groups)]


def _proj(x, in_width, in_block, g, mods_l, mod_row, base, w=None, kinds=(), cs=None, out_dtype=BF16,
          wt=None, kinds_t=(), cst=None):
    bx, n, _ = x.shape
    tm = min(512, n)
    use_mod = mods_l is not None
    kinds, kinds_t = tuple(kinds), tuple(kinds_t)
    in_specs = [
        pl.BlockSpec((None, tm, in_width), lambda b, i: (b, i, in_block)),
        pl.BlockSpec((1, in_width), lambda b, i: (0, 0)),
    ]
    args = [x, g.reshape(1, in_width)]
    if use_mod:
        in_specs.append(pl.BlockSpec((None, N_MOD, in_width), lambda b, i: (mod_row(b), 0, 0)))
        args.append(mods_l)
    p_specs, p_args, out_specs, out_shape = _proj_io(bx, n, tm, in_width, w, kinds, cs, out_dtype, wt, kinds_t, cst)
    outs = pl.pallas_call(
        functools.partial(_proj_body, base=base, use_mod=use_mod, kinds=kinds, kinds_t=kinds_t),
        grid=(bx, n // tm),
        in_specs=in_specs + p_specs,
        out_specs=out_specs,
        out_shape=out_shape,
        compiler_params=_cparams(2),
        name="norm_proj",
    )(*args, *p_args)
    return outs[0] if len(outs) == 1 else outs


def _proj_io(bx, n, tm, in_width, w, kinds, cs, out_dtype, wt, kinds_t, cst):
    in_specs, args, out_specs, out_shape = [], [], [], []
    if kinds:
        f = w.shape[1]
        assert f == len(kinds) * LANES
        in_specs.append(_resident((in_width, f), lambda b, i: (0, 0)))
        args.append(w)
        if any(k is not None for k in kinds):
            in_specs.append(pl.BlockSpec((2, tm, LANES), lambda b, i: (0, i, 0)))
            args.append(cs)
        out_specs.append(pl.BlockSpec((None, tm, f), lambda b, i: (b, i, 0)))
        out_shape.append(jax.ShapeDtypeStruct((bx, n, f), out_dtype))
    if kinds_t:
        fw = wt.shape[0]
        assert fw == len(kinds_t) * HEAD_DIM
        dsts = sorted(dst for outs in kinds_t for dst, _, _ in outs)
        assert dsts == list(range(len(dsts)))
        ft = len(dsts) * HEAD_DIM
        in_specs.append(_resident((fw, in_width), lambda b, i: (0, 0)))
        args.append(wt)
        if _plan_ropes(kinds_t):
            in_specs.append(pl.BlockSpec((2, HEAD_DIM, tm), lambda b, i: (0, 0, i)))
            args.append(cst)
        out_specs.append(pl.BlockSpec((None, ft, tm), lambda b, i: (b, 0, i)))
        out_shape.append(jax.ShapeDtypeStruct((bx, ft, n), BF16))
    return in_specs, args, out_specs, out_shape


def _lat_rows(blk):
    return lambda b, h, i: (b, 0, blk(h))


def _lat_t(blk):
    return lambda b, h, i: (b, blk(h), 0)


def _ctx_t(blk):
    return lambda b, h, i: (0, blk(h), b)


def _mla_body(*refs, tk, has_lat):
    if has_lat:
        qt_ref, kn_ref, kr_ref, vt_ref, knc_ref, krc_ref, vtc_ref, o_ref, s_a, s_b, kcat_ref = refs
    else:
        qt_ref, knc_ref, krc_ref, vtc_ref, o_ref, s_a, s_b = refs
        kcat_ref = vt_ref = None
    qt = qt_ref[...]
    kc = jnp.concatenate([knc_ref[...], krc_ref[...].astype(BF16)], axis=1)
    if has_lat:
        @pl.when(pl.program_id(2) == 0)
        def _():
            kcat_ref[:, :LANES] = kn_ref[...]
            kcat_ref[:, LANES:] = kr_ref[...].astype(BF16)

    st = _attend(qt, kc, vtc_ref[...], (s_a, s_b), qt, kcat_ref, vt_ref, tk)
    o_ref[...] = (st.acc / st.l).T.astype(o_ref.dtype)


def _mla_attn(qt, kvc, p1c, vtc, kv=None, p1=None, vt=None):
    h = MLA_HEADS
    has_lat = kv is not None
    bsz, c = kvc.shape[0], kvc.shape[1]
    lq = qt.shape[2] if has_lat else c
    tq = min(1024, lq)
    rope_tile = (MLA_Q_RANK + MLA_KV_RANK) // LANES
    head = lambda hh: hh
    in_specs, args, scratch, tk = [], [qt], [], 0
    if has_lat:
        lk = kv.shape[1]
        tk = min(512, lk)
        in_specs += [
            pl.BlockSpec((None, 2 * LANES, tq), lambda b, hh, i: (b, hh, i)),
            pl.BlockSpec((None, lk, LANES), _lat_rows(head)),
            pl.BlockSpec((None, lk, LANES), _lat_rows(lambda hh: rope_tile)),
            pl.BlockSpec((None, LANES, lk), _lat_t(head)),
        ]
        args += [kv, p1, vt]
        scratch = [pltpu.VMEM((lk, 2 * LANES), BF16)]
    else:
        in_specs.append(pl.BlockSpec((None, 2 * LANES, tq), _ctx_t(head)))
    in_specs += [
        pl.BlockSpec((None, c, LANES), _lat_rows(head)),
        pl.BlockSpec((None, c, LANES), _lat_rows(lambda hh: rope_tile)),
        pl.BlockSpec((None, LANES, c), _ctx_t(head)),
    ]
    args += [kvc, p1c, vtc]
    return pl.pallas_call(
        functools.partial(_mla_body, tk=tk, has_lat=has_lat),
        grid=(bsz, h, lq // tq),
        in_specs=in_specs,
        out_specs=pl.BlockSpec((None, tq, LANES), lambda b, hh, i: (b, i, hh)),
        out_shape=jax.ShapeDtypeStruct((bsz, lq, h * MLA_V), BF16),
        scratch_shapes=[pltpu.VMEM((max(tk, c), tq), F32)] * 2 + scratch,
        compiler_params=_cparams(3),
        name="mla_attention",
    )(*args)


def _diff_body(*refs, lam_init, tk, has_lat):
    if has_lat:
        qt_ref, k_ref, vt_ref, kc_ref, vtc_ref, lam_ref, ng_ref, o_ref, s_a, s_b = refs
        q_plain, q_rot = _stack_heads_t(qt_ref[LANES:, :]), _stack_heads_t(qt_ref[:LANES, :])
    else:
        qt_ref, kc_ref, vtc_ref, lam_ref, ng_ref, o_ref, s_a, s_b = refs
        q_plain, q_rot, k_ref, vt_ref = _stack_heads_t(qt_ref[...]), None, None, None
    tq = qt_ref.shape[1]
    st = _attend(q_plain, kc_ref[...], vtc_ref[...], (s_a, s_b), q_rot, k_ref, vt_ref, tk)
    lp = lam_ref[...]
    lam = (jnp.exp(jnp.sum(lp[0:1] * lp[1:2], axis=-1, keepdims=True))
           - jnp.exp(jnp.sum(lp[2:3] * lp[3:4], axis=-1, keepdims=True)) + lam_init)
    o = st.acc / st.l
    o = (o[:, :tq] - lam * o[:, tq:]).T
    o_ref[...] = (_rms(o) * ng_ref[...] * (1.0 - lam_init)).astype(o_ref.dtype)


def _diff_attn(kc, ptc, lam_params, norm_g, lam_init, qtc_blk, vtc_blk, kx=None, ptx=None):
    h = DIFF_HEADS
    has_lat = kx is not None
    bsz, c = kc.shape[0], kc.shape[1]
    lq = kx.shape[1] if has_lat else c
    tq = min(512, lq)
    head = lambda hh: hh
    in_specs, args, tk = [], [], 0
    if has_lat:
        lk = kx.shape[1]
        tk = min(512, lk)
        in_specs += [
            pl.BlockSpec((None, 2 * LANES, tq), lambda b, hh, i: (b, hh, i)),
            pl.BlockSpec((None, lk, LANES), _lat_rows(head)),
            pl.BlockSpec((None, LANES, lk), _lat_t(lambda hh: 2 * h + hh)),
        ]
        args += [ptx, kx, ptx]
    else:
        in_specs.append(pl.BlockSpec((None, LANES, tq), _ctx_t(lambda hh: qtc_blk + hh)))
        args.append(ptc)
    in_specs += [
        pl.BlockSpec((None, c, LANES), _lat_rows(head)),
        pl.BlockSpec((None, LANES, c), _ctx_t(lambda hh: vtc_blk + hh)),
        pl.BlockSpec((4, HEAD_DIM), lambda b, hh, i: (0, 0)),
        pl.BlockSpec((1, 2 * HEAD_DIM), lambda b, hh, i: (0, 0)),
    ]
    args += [kc, ptc, lam_params, norm_g.reshape(1, 2 * HEAD_DIM)]
    return pl.pallas_call(
        functools.partial(_diff_body, lam_init=lam_init, tk=tk, has_lat=has_lat),
        grid=(bsz, h, lq // tq),
        in_specs=in_specs,
        out_specs=pl.BlockSpec((None, tq, LANES), lambda b, hh, i: (b, i, hh)),
        out_shape=jax.ShapeDtypeStruct((bsz, lq, h * 2 * HEAD_DIM), BF16),
        scratch_shapes=[pltpu.VMEM((max(tk, c), 2 * tq), F32)] * 2,
        compiler_params=_cparams(3),
        name="diff_attention",
    )(*args)


def _swa_body(*refs, tq, has_lat):
    if has_lat:
        sink_ref, qr_ref, qp_ref, k_ref, vt_ref, kc_ref, vtc_ref, o_ref, s_a, s_b = refs
    else:
        sink_ref, qp_ref, kc_ref, vtc_ref, o_ref, s_a, s_b = refs
    bufs = (s_a, s_b)
    pair = pl.program_id(1)
    kc, vtc = kc_ref[...], vtc_ref[...]
    c = kc.shape[0]
    span = tq + 2 * SWA_WINDOW if has_lat else 0
    first_half = lax.broadcasted_iota(jnp.int32, (1, 2 * tq), 1) < tq
    if has_lat:
        lk = k_ref.shape[0]
        q0 = pl.program_id(2) * tq
        start = pl.multiple_of(jnp.clip(q0 - SWA_WINDOW, 0, lk - span), SWA_WINDOW)
        keys = pl.ds(start, span)
        kpos = start + lax.broadcasted_iota(jnp.int32, (span, 2 * tq), 0)
        qpos = q0 + (lax.broadcasted_iota(jnp.int32, (span, 2 * tq), 1) & (tq - 1))
        valid = jnp.abs(kpos - qpos) <= SWA_WINDOW
    low = _row_is_low((LANES, tq))

    def scores(g):
        rows = slice(g * LANES, (g + 1) * LANES)
        buf = bufs[g % 2]
        buf[0:c, :] = _dot(kc, _stack_heads_t(qp_ref[rows, :]))
        if has_lat:
            s_w = _dot(k_ref[keys, :], _stack_heads_t(qr_ref[rows, :]))
            buf[c:c + span, :] = jnp.where(valid, s_w, NEG_INF)

    scores(0)
    for g in range(SWA_GROUP):
        if g + 1 < SWA_GROUP:
            scores(g + 1)
        sink = jnp.where(first_half, sink_ref[2 * pair * SWA_GROUP + g],
                         sink_ref[(2 * pair + 1) * SWA_GROUP + g]) * LOG2E
        s = bufs[g % 2][...]
        m = jnp.maximum(jnp.max(s, axis=0, keepdims=True), sink)
        p = jnp.exp2(s - m)
        l = jnp.sum(p, axis=0, keepdims=True) + jnp.exp2(sink - m)
        p = p.astype(BF16)
        o = _dot(vtc, p[0:c])
        if has_lat:
            o = o + _dot(vt_ref[:, keys], p[c:])
        o = o / l
        o_ref[:, g * LANES:(g + 1) * LANES] = jnp.where(low, o[:, :tq], o[:, tq:]).T.astype(o_ref.dtype)


def _swa_attn(sink, kc, ptc, kx=None, ptx=None):
    n_pairs = SWA_KV_HEADS // 2
    qw = SWA_GROUP * LANES
    has_lat = kx is not None
    bsz, c = kc.shape[0], kc.shape[1]
    lq = kx.shape[1] if has_lat else c
    tq = min(256, lq)
    pair = lambda p: p
    in_specs = [pl.BlockSpec(memory_space=pltpu.SMEM)]
    args = [sink]
    span = 0
    if has_lat:
        lk = kx.shape[1]
        span = tq + 2 * SWA_WINDOW
        in_specs += [
            pl.BlockSpec((None, qw, tq), lambda b, p, i: (b, p, i)),
            pl.BlockSpec((None, qw, tq), lambda b, p, i: (b, n_pairs + p, i)),
            pl.BlockSpec((None, lk, LANES), _lat_rows(pair)),
            pl.BlockSpec((None, LANES, lk), _lat_t(lambda p: 2 * n_pairs * SWA_GROUP + p)),
        ]
        args += [ptx, ptx, kx, ptx]
    else:
        in_specs.append(pl.BlockSpec((None, qw, tq), _ctx_t(pair)))
        args.append(ptc)
    in_specs += [
        pl.BlockSpec((None, c, LANES), _lat_rows(pair)),
        pl.BlockSpec((None, LANES, c), _ctx_t(lambda p: n_pairs * SWA_GROUP + p)),
    ]
    args += [kc, ptc]
    return pl.pallas_call(
        functools.partial(_swa_body, tq=tq, has_lat=has_lat),
        grid=(bsz, n_pairs, lq // tq),
        in_specs=in_specs,
        out_specs=pl.BlockSpec((None, tq, qw), lambda b, p, i: (b, i, p)),
        out_shape=jax.ShapeDtypeStruct((bsz, lq, n_pairs * qw), BF16),
        scratch_shapes=[pltpu.VMEM((c + span, 2 * tq), F32)] * 2,
        compiler_params=_cparams(3),
        name="swa_attention",
    )(*args)


def _na_body(*refs, n_rows, has_lat):
    if has_lat:
        qt_ref, k_ref, vt_ref, bias_ref, kc_ref, vtc_ref, o_ref, s_a, s_b = refs
    else:
        qt_ref, kc_ref, vtc_ref, o_ref, s_a, s_b = refs
    bufs = (s_a, s_b)
    kc, vtc = kc_ref[...], vtc_ref[...]
    c = kc.shape[0]
    n_steps = qt_ref.shape[1] // LANES
    win = NA_WIN_ROWS * GRID_W
    low = _row_is_low((LANES, LANES))

    def scores(u):
        q = _stack_heads_t(qt_ref[:, u * LANES:(u + 1) * LANES])
        buf = bufs[u % 2]
        buf[0:c, :] = _dot(kc, q)
        if not has_lat:
            return None
        r = (pl.program_id(2) * n_steps + u) * NA_STEP_ROWS
        ws = jnp.clip(r - NA_KH // 2, 0, n_rows - NA_WIN_ROWS)
        cls = lax.shift_right_logical(r - ws, 1)
        keys = pl.ds(pl.multiple_of(ws * GRID_W, LANES), win)
        buf[c:c + win, :] = _dot(k_ref[keys, :], q) + bias_ref[cls]
        return keys

    keys = scores(0)
    for u in range(n_steps):
        next_keys = scores(u + 1) if u + 1 < n_steps else None
        s = bufs[u % 2][...]
        p = jnp.exp2(s - jnp.max(s, axis=0, keepdims=True))
        l = jnp.sum(p, axis=0, keepdims=True)
        p = p.astype(BF16)
        o = _dot(vtc, p[0:c])
        if has_lat:
            o = o + _dot(vt_ref[:, keys], p[c:])
        o = o / l
        o_ref[u * LANES:(u + 1) * LANES, :] = jnp.where(low, o[:, :LANES], o[:, LANES:]).T.astype(o_ref.dtype)
        keys = next_keys


def _na_attn(kc, ptc, bias=None, kx=None, ptx=None):
    n_pairs = NA_HEADS // 2
    has_lat = kx is not None
    bsz, c = kc.shape[0], kc.shape[1]
    pair = lambda p: p
    in_specs, args, n_rows = [], [], 0
    if has_lat:
        lq = kx.shape[1]
        n_rows = lq // GRID_W
        tq = min(1024, lq)
        in_specs += [
            pl.BlockSpec((None, LANES, tq), lambda b, p, i: (b, p, i)),
            pl.BlockSpec((None, lq, LANES), _lat_rows(pair)),
            pl.BlockSpec((None, LANES, lq), _lat_t(lambda p: n_pairs + p)),
            pl.BlockSpec((None, NA_CLASSES, NA_WIN_ROWS * GRID_W, 2 * LANES), lambda b, p, i: (p, 0, 0, 0)),
        ]
        args += [ptx, kx, ptx, bias]
    else:
        lq = tq = c
        in_specs.append(pl.BlockSpec((None, LANES, tq), _ctx_t(pair)))
        args.append(ptc)
    in_specs += [
        pl.BlockSpec((None, c, LANES), _lat_rows(pair)),
        pl.BlockSpec((None, LANES, c), _ctx_t(lambda p: n_pairs + p)),
    ]
    args += [kc, ptc]
    return pl.pallas_call(
        functools.partial(_na_body, n_rows=n_rows, has_lat=has_lat),
        grid=(bsz, n_pairs, lq // tq),
        in_specs=in_specs,
        out_specs=pl.BlockSpec((None, tq, LANES), lambda b, p, i: (b, i, p)),
        out_shape=jax.ShapeDtypeStruct((bsz, lq, NA_HEADS * HEAD_DIM), BF16),
        scratch_shapes=[pltpu.VMEM((c + (NA_WIN_ROWS * GRID_W if has_lat else 0), 2 * LANES), F32)] * 2,
        compiler_params=_cparams(3),
        name="na_attention",
    )(*args)


def _rope_tables(seq):
    t = jnp.arange(seq, dtype=jnp.int32)
    row = (t // GRID_W).astype(F32)
    col = (t % GRID_W).astype(F32)
    n_freq = ROPE_DIM // 4
    inv = jnp.exp(-math.log(ROPE_BASE) * jnp.arange(n_freq, dtype=F32) / n_freq)
    ang = jnp.concatenate([row[:, None] * inv, col[:, None] * inv], axis=-1)
    cos, sin = jnp.cos(ang), jnp.sin(ang)
    c_head = jnp.concatenate([cos, cos], axis=-1)
    s_head = jnp.concatenate([-sin, sin], axis=-1)
    cs = jnp.stack([jnp.concatenate([c_head, c_head], axis=-1), jnp.concatenate([s_head, s_head], axis=-1)])
    cst = jnp.stack([c_head.T, s_head.T])
    return cs, cst


def _na_bias_table(rpb, n_rows):
    assert n_rows % 2 == 0 and n_rows >= NA_WIN_ROWS + 2 and NA_STEP_ROWS == 2
    rel_start = np.array([[0, 0], [0, 0], [0, 1], [2, 2], [2, 2]])
    dq = np.array([[0, 1], [2, 3], [4, 5], [6, 7], [8, 9]])
    jr = np.arange(NA_WIN_ROWS)
    row_ok = (jr >= rel_start[:, :, None]) & (jr < rel_start[:, :, None] + NA_KH)
    ridx = np.clip(jr - dq[:, :, None] + (NA_KH - 1), 0, 2 * NA_KH - 2)
    n_pairs = rpb.shape[0] // 2
    rows = jnp.where(row_ok[None, :, :, :, None], rpb.astype(F32)[:, ridx] * LOG2E, NEG_INF)
    pad = GRID_W - NA_KW
    rows = jnp.pad(rows[..., ::-1], [(0, 0)] * 4 + [(pad, pad + 1)])
    rows = rows.reshape(n_pairs, 2, NA_CLASSES, NA_STEP_ROWS, NA_WIN_ROWS, LANES)
    rows = rows.transpose(0, 2, 1, 3, 4, 5).reshape(n_pairs, NA_CLASSES, -1, LANES)
    return pl.pallas_call(
        _na_bias_body,
        grid=(n_pairs, NA_CLASSES),
        in_specs=[pl.BlockSpec((None, None, rows.shape[2], LANES), lambda p, c: (p, c, 0, 0))],
        out_specs=pl.BlockSpec((None, None, NA_WIN_ROWS * GRID_W, 2 * LANES), lambda p, c: (p, c, 0, 0)),
        out_shape=jax.ShapeDtypeStruct((n_pairs, NA_CLASSES, NA_WIN_ROWS * GRID_W, 2 * LANES), F32),
        compiler_params=_cparams(2),
        name="na_bias_table",
    )(rows)


def _na_bias_body(r_ref, o_ref):
    kcol = lax.broadcasted_iota(jnp.int32, (GRID_W, LANES), 0)
    lane = lax.broadcasted_iota(jnp.int32, (GRID_W, LANES), 1)
    col = lane & (GRID_W - 1)
    col_start = jnp.clip(col - NA_KW // 2, 0, GRID_W - NA_KW)
    in_win = (kcol >= col_start) & (kcol < col_start + NA_KW)
    low = lane < GRID_W
    for e in range(2):
        for jr in range(NA_WIN_ROWS):
            halves = []
            for rr in range(NA_STEP_ROWS):
                t = (e * NA_STEP_ROWS + rr) * NA_WIN_ROWS + jr
                v = jnp.broadcast_to(r_ref[t:t + 1, :], (GRID_W, LANES))
                shift = LANES - (GRID_W - 1) if rr == 0 else LANES - (2 * GRID_W - 1)
                halves.append(pltpu.roll(v, shift, 1, stride=1, stride_axis=0))
            tile = jnp.where(in_win, jnp.where(low, halves[0], halves[1]), NEG_INF)
            o_ref[jr * GRID_W:(jr + 1) * GRID_W, e * LANES:(e + 1) * LANES] = tile


def _swa_q_perm():
    idx = []
    for pair in range(SWA_KV_HEADS // 2):
        for g in range(SWA_GROUP):
            for e in range(2):
                head = (2 * pair + e) * SWA_GROUP + g
                idx.extend(range(head * HEAD_DIM, (head + 1) * HEAD_DIM))
    return np.asarray(idx)


def kernel(x, c, ctx, c_ctx, mod_w, mod_b, norm_g, final_norm_g, ffn_w_in, ffn_w_out,
           mla_w_down, mla_q_norm, mla_kv_norm, mla_w_uq, mla_w_ukv, mla_w_o,
           swa_w_qkv, swa_sink, swa_w_o, na_w_qkv, na_rpb, na_w_o,
           diff_w_qkv, diff_lambda, diff_norm_g, diff_w_o):
    bsz, seq, d = x.shape
    n_ctx = ctx.shape[1]
    depth = mod_w.shape[0]
    assert bsz + 1 <= MOD_ROWS and seq % GRID_W == 0 and n_ctx % LANES == 0

    cond = jnp.zeros((MOD_ROWS, d), F32).at[:bsz].set(c).at[bsz].set(c_ctx)
    mods = _mod_call(cond, mod_w, mod_b).reshape(depth, MOD_ROWS, N_MOD, d)
    cs, cst = _rope_tables(seq)
    w_in, w_out = ffn_w_in.astype(BF16), ffn_w_out.astype(BF16)
    xc = ctx.reshape(1, bsz * n_ctx, d)
    x_row = lambda b: b
    c_row = lambda b: bsz
    c64 = HEAD_DIM ** -0.5 * LOG2E

    def per_batch(t):
        return t.reshape(bsz, n_ctx, t.shape[-1])

    for layer in range(depth):
        kind = layer % 4
        j = layer // 4
        last = layer == depth - 1
        ml = mods[layer]

        def ffn1(t, row, **proj):
            return _ffn(t, norm_g[layer, 0], ml, row, w_in, w_out, layer, 0, 0,
                        proj=dict(g=norm_g[layer, 1], base=3, **proj))

        if kind == 0:
            h = MLA_HEADS
            qk = MLA_Q_RANK + MLA_KV_RANK
            c_mla = (MLA_NOPE + ROPE_DIM) ** -0.5 * LOG2E
            wd = mla_w_down[j].astype(BF16)
            z = jnp.zeros((d, ROPE_DIM), BF16)
            w1x = jnp.concatenate([wd, z], axis=1)
            w1c = jnp.concatenate([wd[:, :qk], z, wd[:, qk:]], axis=1)
            n1 = w1x.shape[1] // LANES
            x, p1x = ffn1(x, x_row, w=w1x, kinds=[None] * (n1 - 1) + [True], cs=cs, out_dtype=F32)
            xc, p1c = ffn1(xc, c_row, w=w1c, kinds=[None] * n1, out_dtype=F32)
            wqt = mla_w_uq[j].astype(BF16).T
            wkv = mla_w_ukv[j].astype(BF16).reshape(MLA_KV_RANK, h, MLA_NOPE + MLA_V)
            wk = wkv[:, :, :MLA_NOPE].reshape(MLA_KV_RANK, -1)
            wvt = wkv[:, :, MLA_NOPE:].reshape(MLA_KV_RANK, -1).T
            kv_blk = MLA_Q_RANK // MLA_KV_RANK
            q_kinds, q_plain = [], []
            for hh in range(h):
                o4 = 4 * hh
                nope = [((o4, False, c_mla),), ((o4 + 1, False, c_mla),)]
                q_kinds += nope + [((o4 + 2, True, c_mla), (o4 + 3, False, c_mla))]
                q_plain += nope + [((o4 + 2, False, c_mla), (o4 + 3, False, c_mla))]
            v_kinds = _plan([(False, 1.0)] * (2 * h))
            k_none = [None] * h
            qtx = _proj(p1x, MLA_Q_RANK, 0, mla_q_norm[j], None, None, 0, wt=wqt, kinds_t=q_kinds, cst=cst)
            kvx, vtx = _proj(p1x, MLA_KV_RANK, kv_blk, mla_kv_norm[j], None, None, 0, w=wk, kinds=k_none,
                             wt=wvt, kinds_t=v_kinds)
            kvc, vtc = _proj(p1c, MLA_KV_RANK, kv_blk, mla_kv_norm[j], None, None, 0, w=wk, kinds=k_none,
                             wt=wvt, kinds_t=v_kinds)
            kvc, p1c_b = per_batch(kvc), per_batch(p1c)
            ox = _mla_attn(qtx, kvc, p1c_b, vtc, kvx, p1x, vtx)
            w_o = mla_w_o[j].astype(BF16)
            if not last:
                qtc = _proj(p1c, MLA_Q_RANK, 0, mla_q_norm[j], None, None, 0, wt=wqt, kinds_t=q_plain)
                oc = _mla_attn(qtc, kvc, p1c_b, vtc)
        elif kind == 1:
            wqkv = swa_w_qkv[j].astype(BF16)
            nq = SWA_KV_HEADS * SWA_GROUP * HEAD_DIM
            nk = SWA_KV_HEADS * HEAD_DIM
            perm = _swa_q_perm()
            wqt = wqkv[:, :nq][:, perm].T
            wk, wvt = wqkv[:, nq:nq + nk], wqkv[:, nq + nk:].T
            gq, gv = nq // HEAD_DIM, nk // HEAD_DIM
            wt = jnp.concatenate([wqt, wvt], axis=0)
            ktx = ([((g, True, c64), (gq + g, False, c64)) for g in range(gq)]
                   + [((2 * gq + g, False, 1.0),) for g in range(gv)])
            x, kx, ptx = ffn1(x, x_row, w=wk, kinds=[True] * (nk // LANES), cs=cs, wt=wt, kinds_t=ktx, cst=cst)
            xc, kc, ptc = ffn1(xc, c_row, w=wk, kinds=[None] * (nk // LANES), wt=wt,
                               kinds_t=_plan([(False, c64)] * gq + [(False, 1.0)] * gv))
            kc = per_batch(kc)
            ox = _swa_attn(swa_sink[j], kc, ptc, kx, ptx)
            w_o = swa_w_o[j].astype(BF16)[perm, :]
            if not last:
                oc = _swa_attn(swa_sink[j], kc, ptc)
        elif kind == 2:
            wqkv = na_w_qkv[j].astype(BF16)
            nh = NA_HEADS * HEAD_DIM
            wk = wqkv[:, nh:2 * nh]
            wt = jnp.concatenate([wqkv[:, :nh], wqkv[:, 2 * nh:]], axis=1).T
            kt = _plan([(False, c64)] * NA_HEADS + [(False, 1.0)] * NA_HEADS)
            k_none = [None] * (nh // LANES)
            x, kx, ptx = ffn1(x, x_row, w=wk, kinds=k_none, wt=wt, kinds_t=kt)
            xc, kc, ptc = ffn1(xc, c_row, w=wk, kinds=k_none, wt=wt, kinds_t=kt)
            kc = per_batch(kc)
            ox = _na_attn(kc, ptc, _na_bias_table(na_rpb[j], seq // GRID_W), kx, ptx)
            w_o = na_w_o[j].astype(BF16)
            if not last:
                oc = _na_attn(kc, ptc)
        else:
            h = DIFF_HEADS
            lam_init = 0.8 - 0.6 * math.exp(-0.3 * layer)
            wqkv = diff_w_qkv[j].astype(BF16)
            nq = 2 * h * HEAD_DIM
            wk = wqkv[:, nq:2 * nq]
            wqt = wqkv[:, :nq].T
            wvt = wqkv[:, 2 * nq:].T
            wt = jnp.concatenate([wqt, wvt], axis=0)
            ktx = ([((4 * (g // 2) + g % 2, True, c64), (4 * (g // 2) + 2 + g % 2, False, c64))
                    for g in range(2 * h)] + [((4 * h + g, False, 1.0),) for g in range(2 * h)])
            ktc = _plan([(False, c64)] * (2 * h) + [(False, 1.0)] * (2 * h))
            x, kx, ptx = ffn1(x, x_row, w=wk, kinds=[True] * h, cs=cs, wt=wt, kinds_t=ktx, cst=cst)
            xc, kc, ptc = ffn1(xc, c_row, w=wk, kinds=[None] * h, wt=wt, kinds_t=ktc)
            kc = per_batch(kc)
            ox = _diff_attn(kc, ptc, diff_lambda[j], diff_norm_g[j], lam_init, 0, h, kx, ptx)
            w_o = diff_w_o[j].astype(BF16)
            if not last:
                oc = _diff_attn(kc, ptc, diff_lambda[j], diff_norm_g[j], lam_init, 0, h)

        x = _ffn(x, norm_g[layer, 2], ml, x_row, w_in, w_out, layer, 1, 6,
                 final_g=final_norm_g if last else None, mixer=(ox, w_o, 5))
        if not last:
            oc = oc.reshape(1, bsz * n_ctx, oc.shape[-1])
            xc = _ffn(xc, norm_g[layer, 2], ml, c_row, w_in, w_out, layer, 1, 6, mixer=(oc, w_o, 5))
    return x
```

```python
import functools
import math

import jax
import jax.numpy as jnp
import numpy as np
from jax import lax
from jax.experimental import pallas as pl
from jax.experimental.pallas import tpu as pltpu

F32 = jnp.float32
BF16 = jnp.bfloat16

GRID_W = 64
N_MOD = 9
EPS = 1e-6
ROPE_BASE = 10000.0
ROPE_DIM = 64
NEG_INF = -1e30
LOG2E = math.log2(math.e)

MLA_HEADS = 8
MLA_Q_RANK = 512
MLA_KV_RANK = 256
MLA_NOPE = 128
MLA_V = 128

SWA_KV_HEADS = 4
SWA_GROUP = 4
SWA_WINDOW = 128

NA_HEADS = 16
NA_KH = 8
NA_KW = 16
NA_STEP_ROWS = 2
NA_WIN_ROWS = NA_KH + NA_STEP_ROWS
NA_CLASSES = 5

DIFF_HEADS = 8
HEAD_DIM = 64

LANES = 128
MXU_WIDTH = 256
VMEM_LIMIT_BYTES = 56 * 1024 * 1024
VMEM_LIMIT_SMALL = 32 * 1024 * 1024
MOD_ROWS = 8


def _cparams(n_axes, vmem_limit=VMEM_LIMIT_SMALL):
    return pltpu.CompilerParams(
        dimension_semantics=("arbitrary",) * n_axes,
        vmem_limit_bytes=vmem_limit,
    )


def _resident(shape, index_map):
    return pl.BlockSpec(shape, index_map, pipeline_mode=pl.Buffered(1))


def _rms(x):
    return x * lax.rsqrt(jnp.mean(x * x, axis=-1, keepdims=True) + EPS)


def _norm_mod(x, g, mod_ref, base):
    y = _rms(x) * g
    if mod_ref is not None:
        y = y * (1.0 + mod_ref[base + 1:base + 2, :]) + mod_ref[base:base + 1, :]
    return y


def _dot(a, b):
    return jnp.dot(a, b, preferred_element_type=F32)


def _dot_nt(a, b):
    return lax.dot_general(a, b, (((1,), (1,)), ((), ())), preferred_element_type=F32)


def _row_is_low(shape):
    return lax.broadcasted_iota(jnp.int32, shape, 0) < HEAD_DIM


def _stack_heads_t(qt):
    low = _row_is_low(qt.shape)
    zero = jnp.zeros_like(qt)
    return jnp.concatenate([jnp.where(low, qt, zero), jnp.where(low, zero, qt)], axis=1)


def _rope_tile(y, c, s):
    lane = lax.broadcasted_iota(jnp.int32, y.shape, 1)
    first = (lane & (HEAD_DIM - 1)) < HEAD_DIM // 2
    swapped = jnp.where(first, pltpu.roll(y, LANES - HEAD_DIM // 2, 1), pltpu.roll(y, HEAD_DIM // 2, 1))
    return y * c + swapped * s


class _SoftmaxT:
    def __init__(self, m, l, acc):
        self.m, self.l, self.acc = m, l, acc

    @staticmethod
    def first(s, vt):
        m = jnp.max(s, axis=0, keepdims=True)
        p = jnp.exp2(s - m)
        return _SoftmaxT(m, jnp.sum(p, axis=0, keepdims=True), _dot(vt, p.astype(vt.dtype)))

    def update(self, s, vt):
        m_new = jnp.maximum(self.m, jnp.max(s, axis=0, keepdims=True))
        a = jnp.exp2(self.m - m_new)
        p = jnp.exp2(s - m_new)
        return _SoftmaxT(m_new, a * self.l + jnp.sum(p, axis=0, keepdims=True),
                         a * self.acc + _dot(vt, p.astype(vt.dtype)))


def _attend(qt_ctx, kc, vtc, bufs, qt_lat=None, k_ref=None, vt_ref=None, tk=0):
    c = kc.shape[0]
    n = 1 + (k_ref.shape[0] // tk if k_ref is not None else 0)
    width = qt_ctx.shape[1]
    n_split = 2 if width % (2 * MXU_WIDTH) == 0 else 1
    cols = [slice(u * width // n_split, (u + 1) * width // n_split) for u in range(n_split)]

    def write_scores(j, u):
        if j == 0:
            bufs[0][0:c, cols[u]] = _dot(kc, qt_ctx[:, cols[u]])
        else:
            bufs[j % 2][0:tk, cols[u]] = _dot(k_ref[(j - 1) * tk:j * tk, :], qt_lat[:, cols[u]])

    for u in range(n_split):
        write_scores(0, u)
    sts = [None] * n_split
    for j in range(n):
        for u in range(n_split):
            if j + 1 < n:
                write_scores(j + 1, u)
            if j == 0:
                sts[u] = _SoftmaxT.first(bufs[0][0:c, cols[u]], vtc)
            else:
                sts[u] = sts[u].update(bufs[j % 2][0:tk, cols[u]], vt_ref[:, (j - 1) * tk:j * tk])
    if n_split == 1:
        return sts[0]
    return _SoftmaxT(*(jnp.concatenate([getattr(st, f) for st in sts], axis=1) for f in ("m", "l", "acc")))


def _mod_body(c_ref, w_ref, b_ref, o_ref):
    c = c_ref[...]
    h = (c * jax.nn.sigmoid(c)).astype(BF16)
    o_ref[...] = _dot(h, w_ref[...].astype(BF16)) + b_ref[...]


def _mod_call(cond, mod_w, mod_b):
    depth, d, n = mod_w.shape
    tn = 1024
    return pl.pallas_call(
        _mod_body,
        grid=(depth, n // tn),
        in_specs=[
            pl.BlockSpec((MOD_ROWS, d), lambda l, j: (0, 0)),
            pl.BlockSpec((None, d, tn), lambda l, j: (l, 0, j)),
            pl.BlockSpec((None, 1, tn), lambda l, j: (l, 0, j)),
        ],
        out_specs=pl.BlockSpec((None, MOD_ROWS, tn), lambda l, j: (l, 0, j)),
        out_shape=jax.ShapeDtypeStruct((depth, MOD_ROWS, n), F32),
        compiler_params=_cparams(2),
        name="adaln_mod",
    )(cond, mod_w, mod_b.reshape(depth, 1, n))


def _ffn_body(x_ref, g_ref, mod_ref, win_ref, wout_ref, *rest, base, ff, chunk, final, mixer_base, proj):
    it = iter(rest[:-2])
    o_ref, act_ref = rest[-2:]
    a_ref, wo_ref = (next(it), next(it)) if mixer_base is not None else (None, None)
    fg_ref = next(it) if final else None
    x = x_ref[...]
    if mixer_base is not None:
        x = x + mod_ref[mixer_base:mixer_base + 1, :] * _dot(a_ref[...], wo_ref[...])
    hb = _norm_mod(x, g_ref[...], mod_ref, base).astype(BF16)
    for c0 in range(0, ff, chunk):
        g = _dot(hb, win_ref[:, c0:c0 + chunk])
        u = _dot(hb, win_ref[:, ff + c0:ff + c0 + chunk])
        act_ref[:, c0:c0 + chunk] = (g * jax.nn.sigmoid(g) * u).astype(BF16)
    y = x + (0.5 * mod_ref[base + 2:base + 3, :]) * _dot(act_ref[...], wout_ref[...])
    if final:
        y = _rms(y) * fg_ref[...]
    o_ref[...] = y
    if proj is not None:
        proj_base, kinds, kinds_t = proj
        g2_ref = next(it)
        _proj_emit(_norm_mod(y, g2_ref[...], mod_ref, proj_base).astype(BF16), it, kinds, kinds_t)


def _ffn(x, g, mods_l, mod_row, w_in, w_out, layer, half, base, final_g=None, mixer=None, proj=None):
    bx, n, d = x.shape
    ff = w_out.shape[2]
    tm = min(512, n)
    final = final_g is not None
    in_specs = [
        pl.BlockSpec((None, tm, d), lambda b, i: (b, i, 0)),
        pl.BlockSpec((1, d), lambda b, i: (0, 0)),
        pl.BlockSpec((None, N_MOD, d), lambda b, i: (mod_row(b), 0, 0)),
        _resident((None, None, d, 2 * ff), lambda b, i: (layer, half, 0, 0)),
        _resident((None, None, ff, d), lambda b, i: (layer, half, 0, 0)),
    ]
    args = [x, g.reshape(1, d), mods_l, w_in, w_out]
    mixer_base = None
    if mixer is not None:
        a, w_o, mixer_base = mixer
        k = a.shape[-1]
        in_specs += [pl.BlockSpec((None, tm, k), lambda b, i: (b, i, 0)), _resident((k, d), lambda b, i: (0, 0))]
        args += [a, w_o]
    if final:
        in_specs.append(pl.BlockSpec((1, d), lambda b, i: (0, 0)))
        args.append(final_g.reshape(1, d))
    out_specs, out_shape, proj_static = [], [], None
    if proj is not None:
        kinds, kinds_t = tuple(proj.get("kinds", ())), tuple(proj.get("kinds_t", ()))
        p_specs, p_args, out_specs, out_shape = _proj_io(
            bx, n, tm, d, proj.get("w"), kinds, proj.get("cs"), proj.get("out_dtype", BF16),
            proj.get("wt"), kinds_t, proj.get("cst"))
        in_specs += [pl.BlockSpec((1, d), lambda b, i: (0, 0))] + p_specs
        args += [proj["g"].reshape(1, d)] + p_args
        proj_static = (proj["base"], kinds, kinds_t)
    outs = pl.pallas_call(
        functools.partial(_ffn_body, base=base, ff=ff, chunk=MXU_WIDTH, final=final, mixer_base=mixer_base,
                          proj=proj_static),
        grid=(bx, n // tm),
        in_specs=in_specs,
        out_specs=out_specs + [pl.BlockSpec((None, tm, d), lambda b, i: (b, i, 0))],
        out_shape=out_shape + [jax.ShapeDtypeStruct(x.shape, F32)],
        scratch_shapes=[pltpu.VMEM((tm, ff), BF16)],
        compiler_params=_cparams(2, VMEM_LIMIT_BYTES),
        name="ffn_swiglu",
    )(*args)
    return outs[-1] if proj is None else (outs[-1], *outs[:-1])


PROJ_T_ROWS = 512


def _proj_body(*refs, base, use_mod, kinds, kinds_t):
    it = iter(refs)
    x_ref, g_ref = next(it), next(it)
    mod_ref = next(it) if use_mod else None
    hb = _norm_mod(x_ref[...].astype(F32), g_ref[...], mod_ref, base).astype(BF16)
    _proj_emit(hb, it, kinds, kinds_t)


def _proj_emit(hb, it, kinds, kinds_t):
    w_ref = next(it) if kinds else None
    cs_ref = next(it) if any(k is not None for k in kinds) else None
    wt_ref = next(it) if kinds_t else None
    cst_ref = next(it) if _plan_ropes(kinds_t) else None
    o_ref = next(it) if kinds else None
    ot_ref = next(it) if kinds_t else None

    n_tiles = len(kinds)
    t = 0
    while t < n_tiles:
        w = min(MXU_WIDTH // LANES, n_tiles - t)
        y = _dot(hb, w_ref[:, t * LANES:(t + w) * LANES])
        for j in range(w):
            yj = y[:, j * LANES:(j + 1) * LANES]
            if kinds[t + j] is not None:
                yj = _rope_tile(yj, cs_ref[0], cs_ref[1])
            o_ref[:, (t + j) * LANES:(t + j + 1) * LANES] = yj.astype(o_ref.dtype)
        t += w

    half = HEAD_DIM // 2
    n_rows_t = len(kinds_t) * HEAD_DIM
    for r0 in range(0, n_rows_t, PROJ_T_ROWS):
        rows = min(PROJ_T_ROWS, n_rows_t - r0)
        yt = _dot_nt(wt_ref[r0:r0 + rows, :], hb)
        for gi in range(rows // HEAD_DIM):
            src = yt[gi * HEAD_DIM:(gi + 1) * HEAD_DIM]
            for dst, rope, scale in kinds_t[r0 // HEAD_DIM + gi]:
                tile = src
                if rope:
                    swapped = jnp.concatenate([src[half:], src[:half]], axis=0)
                    tile = src * cst_ref[0] + swapped * cst_ref[1]
                if scale != 1.0:
                    tile = tile * scale
                ot_ref[dst * HEAD_DIM:(dst + 1) * HEAD_DIM, :] = tile.astype(ot_ref.dtype)


def _plan_ropes(plan):
    return any(rope for outs in plan for _, rope, _ in outs)


def _plan(groups):
    return [((g, rope, scale),) for g, (rope, scale) in enumerate(groups)]


def _proj(x, in_width, in_block, g, mods_l, mod_row, base, w=None, kinds=(), cs=None, out_dtype=BF16,
          wt=None, kinds_t=(), cst=None):
    bx, n, _ = x.shape
    tm = min(512, n)
    use_mod = mods_l is not None
    kinds, kinds_t = tuple(kinds), tuple(kinds_t)
    in_specs = [
        pl.BlockSpec((None, tm, in_width), lambda b, i: (b, i, in_block)),
        pl.BlockSpec((1, in_width), lambda b, i: (0, 0)),
    ]
    args = [x, g.reshape(1, in_width)]
    if use_mod:
        in_specs.append(pl.BlockSpec((None, N_MOD, in_width), lambda b, i: (mod_row(b), 0, 0)))
        args.append(mods_l)
    p_specs, p_args, out_specs, out_shape = _proj_io(bx, n, tm, in_width, w, kinds, cs, out_dtype, wt, kinds_t, cst)
    outs = pl.pallas_call(
        functools.partial(_proj_body, base=base, use_mod=use_mod, kinds=kinds, kinds_t=kinds_t),
        grid=(bx, n // tm),
        in_specs=in_specs + p_specs,
        out_specs=out_specs,
        out_shape=out_shape,
        compiler_params=_cparams(2),
        name="norm_proj",
    )(*args, *p_args)
    return outs[0] if len(outs) == 1 else outs


def _proj_io(bx, n, tm, in_width, w, kinds, cs, out_dtype, wt, kinds_t, cst):
    in_specs, args, out_specs, out_shape = [], [], [], []
    if kinds:
        f = w.shape[1]
        assert f == len(kinds) * LANES
        in_specs.append(_resident((in_width, f), lambda b, i: (0, 0)))
        args.append(w)
        if any(k is not None for k in kinds):
            in_specs.append(pl.BlockSpec((2, tm, LANES), lambda b, i: (0, i, 0)))
            args.append(cs)
        out_specs.append(pl.BlockSpec((None, tm, f), lambda b, i: (b, i, 0)))
        out_shape.append(jax.ShapeDtypeStruct((bx, n, f), out_dtype))
    if kinds_t:
        fw = wt.shape[0]
        assert fw == len(kinds_t) * HEAD_DIM
        dsts = sorted(dst for outs in kinds_t for dst, _, _ in outs)
        assert dsts == list(range(len(dsts)))
        ft = len(dsts) * HEAD_DIM
        in_specs.append(_resident((fw, in_width), lambda b, i: (0, 0)))
        args.append(wt)
        if _plan_ropes(kinds_t):
            in_specs.append(pl.BlockSpec((2, HEAD_DIM, tm), lambda b, i: (0, 0, i)))
            args.append(cst)
        out_specs.append(pl.BlockSpec((None, ft, tm), lambda b, i: (b, 0, i)))
        out_shape.append(jax.ShapeDtypeStruct((bx, ft, n), BF16))
    return in_specs, args, out_specs, out_shape


def _lat_rows(blk):
    return lambda b, h, i: (b, 0, blk(h))


def _lat_t(blk):
    return lambda b, h, i: (b, blk(h), 0)


def _ctx_t(blk):
    return lambda b, h, i: (0, blk(h), b)


def _mla_body(*refs, tk, has_lat):
    if has_lat:
        qt_ref, kn_ref, kr_ref, vt_ref, knc_ref, krc_ref, vtc_ref, o_ref, s_a, s_b, kcat_ref = refs
    else:
        qt_ref, knc_ref, krc_ref, vtc_ref, o_ref, s_a, s_b = refs
        kcat_ref = vt_ref = None
    qt = qt_ref[...]
    kc = jnp.concatenate([knc_ref[...], krc_ref[...].astype(BF16)], axis=1)
    if has_lat:
        @pl.when(pl.program_id(2) == 0)
        def _():
            kcat_ref[:, :LANES] = kn_ref[...]
            kcat_ref[:, LANES:] = kr_ref[...].astype(BF16)

    st = _attend(qt, kc, vtc_ref[...], (s_a, s_b), qt, kcat_ref, vt_ref, tk)
    o_ref[...] = (st.acc / st.l).T.astype(o_ref.dtype)


def _mla_attn(qt, kvc, p1c, vtc, kv=None, p1=None, vt=None):
    h = MLA_HEADS
    has_lat = kv is not None
    bsz, c = kvc.shape[0], kvc.shape[1]
    lq = qt.shape[2] if has_lat else c
    tq = min(1024, lq)
    rope_tile = (MLA_Q_RANK + MLA_KV_RANK) // LANES
    head = lambda hh: hh
    in_specs, args, scratch, tk = [], [qt], [], 0
    if has_lat:
        lk = kv.shape[1]
        tk = min(512, lk)
        in_specs += [
            pl.BlockSpec((None, 2 * LANES, tq), lambda b, hh, i: (b, hh, i)),
            pl.BlockSpec((None, lk, LANES), _lat_rows(head)),
            pl.BlockSpec((None, lk, LANES), _lat_rows(lambda hh: rope_tile)),
            pl.BlockSpec((None, LANES, lk), _lat_t(head)),
        ]
        args += [kv, p1, vt]
        scratch = [pltpu.VMEM((lk, 2 * LANES), BF16)]
    else:
        in_specs.append(pl.BlockSpec((None, 2 * LANES, tq), _ctx_t(head)))
    in_specs += [
        pl.BlockSpec((None, c, LANES), _lat_rows(head)),
        pl.BlockSpec((None, c, LANES), _lat_rows(lambda hh: rope_tile)),
        pl.BlockSpec((None, LANES, c), _ctx_t(head)),
    ]
    args += [kvc, p1c, vtc]
    return pl.pallas_call(
        functools.partial(_mla_body, tk=tk, has_lat=has_lat),
        grid=(bsz, h, lq // tq),
        in_specs=in_specs,
        out_specs=pl.BlockSpec((None, tq, LANES), lambda b, hh, i: (b, i, hh)),
        out_shape=jax.ShapeDtypeStruct((bsz, lq, h * MLA_V), BF16),
        scratch_shapes=[pltpu.VMEM((max(tk, c), tq), F32)] * 2 + scratch,
        compiler_params=_cparams(3),
        name="mla_attention",
    )(*args)


def _diff_body(*refs, lam_init, tk, has_lat):
    if has_lat:
        qt_ref, k_ref, vt_ref, kc_ref, vtc_ref, lam_ref, ng_ref, o_ref, s_a, s_b = refs
        q_plain, q_rot = _stack_heads_t(qt_ref[LANES:, :]), _stack_heads_t(qt_ref[:LANES, :])
    else:
        qt_ref, kc_ref, vtc_ref, lam_ref, ng_ref, o_ref, s_a, s_b = refs
        q_plain, q_rot, k_ref, vt_ref = _stack_heads_t(qt_ref[...]), None, None, None
    tq = qt_ref.shape[1]
    st = _attend(q_plain, kc_ref[...], vtc_ref[...], (s_a, s_b), q_rot, k_ref, vt_ref, tk)
    lp = lam_ref[...]
    lam = (jnp.exp(jnp.sum(lp[0:1] * lp[1:2], axis=-1, keepdims=True))
           - jnp.exp(jnp.sum(lp[2:3] * lp[3:4], axis=-1, keepdims=True)) + lam_init)
    o = st.acc / st.l
    o = (o[:, :tq] - lam * o[:, tq:]).T
    o_ref[...] = (_rms(o) * ng_ref[...] * (1.0 - lam_init)).astype(o_ref.dtype)


def _diff_attn(kc, ptc, lam_params, norm_g, lam_init, qtc_blk, vtc_blk, kx=None, ptx=None):
    h = DIFF_HEADS
    has_lat = kx is not None
    bsz, c = kc.shape[0], kc.shape[1]
    lq = kx.shape[1] if has_lat else c
    tq = min(512, lq)
    head = lambda hh: hh
    in_specs, args, tk = [], [], 0
    if has_lat:
        lk = kx.shape[1]
        tk = min(512, lk)
        in_specs += [
            pl.BlockSpec((None, 2 * LANES, tq), lambda b, hh, i: (b, hh, i)),
            pl.BlockSpec((None, lk, LANES), _lat_rows(head)),
            pl.BlockSpec((None, LANES, lk), _lat_t(lambda hh: 2 * h + hh)),
        ]
        args += [ptx, kx, ptx]
    else:
        in_specs.append(pl.BlockSpec((None, LANES, tq), _ctx_t(lambda hh: qtc_blk + hh)))
        args.append(ptc)
    in_specs += [
        pl.BlockSpec((None, c, LANES), _lat_rows(head)),
        pl.BlockSpec((None, LANES, c), _ctx_t(lambda hh: vtc_blk + hh)),
        pl.BlockSpec((4, HEAD_DIM), lambda b, hh, i: (0, 0)),
        pl.BlockSpec((1, 2 * HEAD_DIM), lambda b, hh, i: (0, 0)),
    ]
    args += [kc, ptc, lam_params, norm_g.reshape(1, 2 * HEAD_DIM)]
    return pl.pallas_call(
        functools.partial(_diff_body, lam_init=lam_init, tk=tk, has_lat=has_lat),
        grid=(bsz, h, lq // tq),
        in_specs=in_specs,
        out_specs=pl.BlockSpec((None, tq, LANES), lambda b, hh, i: (b, i, hh)),
        out_shape=jax.ShapeDtypeStruct((bsz, lq, h * 2 * HEAD_DIM), BF16),
        scratch_shapes=[pltpu.VMEM((max(tk, c), 2 * tq), F32)] * 2,
        compiler_params=_cparams(3),
        name="diff_attention",
    )(*args)


def _swa_body(*refs, tq, has_lat):
    if has_lat:
        sink_ref, qr_ref, qp_ref, k_ref, vt_ref, kc_ref, vtc_ref, o_ref, s_a, s_b = refs
    else:
        sink_ref, qp_ref, kc_ref, vtc_ref, o_ref, s_a, s_b = refs
    bufs = (s_a, s_b)
    pair = pl.program_id(1)
    kc, vtc = kc_ref[...], vtc_ref[...]
    c = kc.shape[0]
    span = tq + 2 * SWA_WINDOW if has_lat else 0
    first_half = lax.broadcasted_iota(jnp.int32, (1, 2 * tq), 1) < tq
    if has_lat:
        lk = k_ref.shape[0]
        q0 = pl.program_id(2) * tq
        start = pl.multiple_of(jnp.clip(q0 - SWA_WINDOW, 0, lk - span), SWA_WINDOW)
        keys = pl.ds(start, span)
        kpos = start + lax.broadcasted_iota(jnp.int32, (span, 2 * tq), 0)
        qpos = q0 + (lax.broadcasted_iota(jnp.int32, (span, 2 * tq), 1) & (tq - 1))
        valid = jnp.abs(kpos - qpos) <= SWA_WINDOW
    low = _row_is_low((LANES, tq))

    def scores(g):
        rows = slice(g * LANES, (g + 1) * LANES)
        buf = bufs[g % 2]
        buf[0:c, :] = _dot(kc, _stack_heads_t(qp_ref[rows, :]))
        if has_lat:
            s_w = _dot(k_ref[keys, :], _stack_heads_t(qr_ref[rows, :]))
            buf[c:c + span, :] = jnp.where(valid, s_w, NEG_INF)

    scores(0)
    for g in range(SWA_GROUP):
        if g + 1 < SWA_GROUP:
            scores(g + 1)
        sink = jnp.where(first_half, sink_ref[2 * pair * SWA_GROUP + g],
                         sink_ref[(2 * pair + 1) * SWA_GROUP + g]) * LOG2E
        s = bufs[g % 2][...]
        m = jnp.maximum(jnp.max(s, axis=0, keepdims=True), sink)
        p = jnp.exp2(s - m)
        l = jnp.sum(p, axis=0, keepdims=True) + jnp.exp2(sink - m)
        p = p.astype(BF16)
        o = _dot(vtc, p[0:c])
        if has_lat:
            o = o + _dot(vt_ref[:, keys], p[c:])
        o = o / l
        o_ref[:, g * LANES:(g + 1) * LANES] = jnp.where(low, o[:, :tq], o[:, tq:]).T.astype(o_ref.dtype)


def _swa_attn(sink, kc, ptc, kx=None, ptx=None):
    n_pairs = SWA_KV_HEADS // 2
    qw = SWA_GROUP * LANES
    has_lat = kx is not None
    bsz, c = kc.shape[0], kc.shape[1]
    lq = kx.shape[1] if has_lat else c
    tq = min(256, lq)
    pair = lambda p: p
    in_specs = [pl.BlockSpec(memory_space=pltpu.SMEM)]
    args = [sink]
    span = 0
    if has_lat:
        lk = kx.shape[1]
        span = tq + 2 * SWA_WINDOW
        in_specs += [
            pl.BlockSpec((None, qw, tq), lambda b, p, i: (b, p, i)),
            pl.BlockSpec((None, qw, tq), lambda b, p, i: (b, n_pairs + p, i)),
            pl.BlockSpec((None, lk, LANES), _lat_rows(pair)),
            pl.BlockSpec((None, LANES, lk), _lat_t(lambda p: 2 * n_pairs * SWA_GROUP + p)),
        ]
        args += [ptx, ptx, kx, ptx]
    else:
        in_specs.append(pl.BlockSpec((None, qw, tq), _ctx_t(pair)))
        args.append(ptc)
    in_specs += [
        pl.BlockSpec((None, c, LANES), _lat_rows(pair)),
        pl.BlockSpec((None, LANES, c), _ctx_t(lambda p: n_pairs * SWA_GROUP + p)),
    ]
    args += [kc, ptc]
    return pl.pallas_call(
        functools.partial(_swa_body, tq=tq, has_lat=has_lat),
        grid=(bsz, n_pairs, lq // tq),
        in_specs=in_specs,
        out_specs=pl.BlockSpec((None, tq, qw), lambda b, p, i: (b, i, p)),
        out_shape=jax.ShapeDtypeStruct((bsz, lq, n_pairs * qw), BF16),
        scratch_shapes=[pltpu.VMEM((c + span, 2 * tq), F32)] * 2,
        compiler_params=_cparams(3),
        name="swa_attention",
    )(*args)


def _na_body(*refs, n_rows, has_lat):
    if has_lat:
        qt_ref, k_ref, vt_ref, bias_ref, kc_ref, vtc_ref, o_ref, s_a, s_b = refs
    else:
        qt_ref, kc_ref, vtc_ref, o_ref, s_a, s_b = refs
    bufs = (s_a, s_b)
    kc, vtc = kc_ref[...], vtc_ref[...]
    c = kc.shape[0]
    n_steps = qt_ref.shape[1] // LANES
    win = NA_WIN_ROWS * GRID_W
    low = _row_is_low((LANES, LANES))

    def scores(u):
        q = _stack_heads_t(qt_ref[:, u * LANES:(u + 1) * LANES])
        buf = bufs[u % 2]
        buf[0:c, :] = _dot(kc, q)
        if not has_lat:
            return None
        r = (pl.program_id(2) * n_steps + u) * NA_STEP_ROWS
        ws = jnp.clip(r - NA_KH // 2, 0, n_rows - NA_WIN_ROWS)
        cls = lax.shift_right_logical(r - ws, 1)
        keys = pl.ds(pl.multiple_of(ws * GRID_W, LANES), win)
        buf[c:c + win, :] = _dot(k_ref[keys, :], q) + bias_ref[cls]
        return keys

    keys = scores(0)
    for u in range(n_steps):
        next_keys = scores(u + 1) if u + 1 < n_steps else None
        s = bufs[u % 2][...]
        p = jnp.exp2(s - jnp.max(s, axis=0, keepdims=True))
        l = jnp.sum(p, axis=0, keepdims=True)
        p = p.astype(BF16)
        o = _dot(vtc, p[0:c])
        if has_lat:
            o = o + _dot(vt_ref[:, keys], p[c:])
        o = o / l
        o_ref[u * LANES:(u + 1) * LANES, :] = jnp.where(low, o[:, :LANES], o[:, LANES:]).T.astype(o_ref.dtype)
        keys = next_keys


def _na_attn(kc, ptc, bias=None, kx=None, ptx=None):
    n_pairs = NA_HEADS // 2
    has_lat = kx is not None
    bsz, c = kc.shape[0], kc.shape[1]
    pair = lambda p: p
    in_specs, args, n_rows = [], [], 0
    if has_lat:
        lq = kx.shape[1]
        n_rows = lq // GRID_W
        tq = min(1024, lq)
        in_specs += [
            pl.BlockSpec((None, LANES, tq), lambda b, p, i: (b, p, i)),
            pl.BlockSpec((None, lq, LANES), _lat_rows(pair)),
            pl.BlockSpec((None, LANES, lq), _lat_t(lambda p: n_pairs + p)),
            pl.BlockSpec((None, NA_CLASSES, NA_WIN_ROWS * GRID_W, 2 * LANES), lambda b, p, i: (p, 0, 0, 0)),
        ]
        args += [ptx, kx, ptx, bias]
    else:
        lq = tq = c
        in_specs.append(pl.BlockSpec((None, LANES, tq), _ctx_t(pair)))
        args.append(ptc)
    in_specs += [
        pl.BlockSpec((None, c, LANES), _lat_rows(pair)),
        pl.BlockSpec((None, LANES, c), _ctx_t(lambda p: n_pairs + p)),
    ]
    args += [kc, ptc]
    return pl.pallas_call(
        functools.partial(_na_body, n_rows=n_rows, has_lat=has_lat),
        grid=(bsz, n_pairs, lq // tq),
        in_specs=in_specs,
        out_specs=pl.BlockSpec((None, tq, LANES), lambda b, p, i: (b, i, p)),
        out_shape=jax.ShapeDtypeStruct((bsz, lq, NA_HEADS * HEAD_DIM), BF16),
        scratch_shapes=[pltpu.VMEM((c + (NA_WIN_ROWS * GRID_W if has_lat else 0), 2 * LANES), F32)] * 2,
        compiler_params=_cparams(3),
        name="na_attention",
    )(*args)


def _rope_tables(seq):
    t = jnp.arange(seq, dtype=jnp.int32)
    row = (t // GRID_W).astype(F32)
    col = (t % GRID_W).astype(F32)
    n_freq = ROPE_DIM // 4
    inv = jnp.exp(-math.log(ROPE_BASE) * jnp.arange(n_freq, dtype=F32) / n_freq)
    ang = jnp.concatenate([row[:, None] * inv, col[:, None] * inv], axis=-1)
    cos, sin = jnp.cos(ang), jnp.sin(ang)
    c_head = jnp.concatenate([cos, cos], axis=-1)
    s_head = jnp.concatenate([-sin, sin], axis=-1)
    cs = jnp.stack([jnp.concatenate([c_head, c_head], axis=-1), jnp.concatenate([s_head, s_head], axis=-1)])
    cst = jnp.stack([c_head.T, s_head.T])
    return cs, cst


def _na_bias_table(rpb, n_rows):
    assert n_rows % 2 == 0 and n_rows >= NA_WIN_ROWS + 2 and NA_STEP_ROWS == 2
    rel_start = np.array([[0, 0], [0, 0], [0, 1], [2, 2], [2, 2]])
    dq = np.array([[0, 1], [2, 3], [4, 5], [6, 7], [8, 9]])
    jr = np.arange(NA_WIN_ROWS)
    row_ok = (jr >= rel_start[:, :, None]) & (jr < rel_start[:, :, None] + NA_KH)
    ridx = np.clip(jr - dq[:, :, None] + (NA_KH - 1), 0, 2 * NA_KH - 2)
    n_pairs = rpb.shape[0] // 2
    rows = jnp.where(row_ok[None, :, :, :, None], rpb.astype(F32)[:, ridx] * LOG2E, NEG_INF)
    pad = GRID_W - NA_KW
    rows = jnp.pad(rows[..., ::-1], [(0, 0)] * 4 + [(pad, pad + 1)])
    rows = rows.reshape(n_pairs, 2, NA_CLASSES, NA_STEP_ROWS, NA_WIN_ROWS, LANES)
    rows = rows.transpose(0, 2, 1, 3, 4, 5).reshape(n_pairs, NA_CLASSES, -1, LANES)
    return pl.pallas_call(
        _na_bias_body,
        grid=(n_pairs, NA_CLASSES),
        in_specs=[pl.BlockSpec((None, None, rows.shape[2], LANES), lambda p, c: (p, c, 0, 0))],
        out_specs=pl.BlockSpec((None, None, NA_WIN_ROWS * GRID_W, 2 * LANES), lambda p, c: (p, c, 0, 0)),
        out_shape=jax.ShapeDtypeStruct((n_pairs, NA_CLASSES, NA_WIN_ROWS * GRID_W, 2 * LANES), F32),
        compiler_params=_cparams(2),
        name="na_bias_table",
    )(rows)


def _na_bias_body(r_ref, o_ref):
    kcol = lax.broadcasted_iota(jnp.int32, (GRID_W, LANES), 0)
    lane = lax.broadcasted_iota(jnp.int32, (GRID_W, LANES), 1)
    col = lane & (GRID_W - 1)
    col_start = jnp.clip(col - NA_KW // 2, 0, GRID_W - NA_KW)
    in_win = (kcol >= col_start) & (kcol < col_start + NA_KW)
    low = lane < GRID_W
    for e in range(2):
        for jr in range(NA_WIN_ROWS):
            halves = []
            for rr in range(NA_STEP_ROWS):
                t = (e * NA_STEP_ROWS + rr) * NA_WIN_ROWS + jr
                v = jnp.broadcast_to(r_ref[t:t + 1, :], (GRID_W, LANES))
                shift = LANES - (GRID_W - 1) if rr == 0 else LANES - (2 * GRID_W - 1)
                halves.append(pltpu.roll(v, shift, 1, stride=1, stride_axis=0))
            tile = jnp.where(in_win, jnp.where(low, halves[0], halves[1]), NEG_INF)
            o_ref[jr * GRID_W:(jr + 1) * GRID_W, e * LANES:(e + 1) * LANES] = tile


def _swa_q_perm():
    idx = []
    for pair in range(SWA_KV_HEADS // 2):
        for g in range(SWA_GROUP):
            for e in range(2):
                head = (2 * pair + e) * SWA_GROUP + g
                idx.extend(range(head * HEAD_DIM, (head + 1) * HEAD_DIM))
    return np.asarray(idx)


def kernel(x, c, ctx, c_ctx, mod_w, mod_b, norm_g, final_norm_g, ffn_w_in, ffn_w_out,
           mla_w_down, mla_q_norm, mla_kv_norm, mla_w_uq, mla_w_ukv, mla_w_o,
           swa_w_qkv, swa_sink, swa_w_o, na_w_qkv, na_rpb, na_w_o,
           diff_w_qkv, diff_lambda, diff_norm_g, diff_w_o):
    bsz, seq, d = x.shape
    n_ctx = ctx.shape[1]
    depth = mod_w.shape[0]
    assert bsz + 1 <= MOD_ROWS and seq % GRID_W == 0 and n_ctx % LANES == 0

    cond = jnp.zeros((MOD_ROWS, d), F32).at[:bsz].set(c).at[bsz].set(c_ctx)
    mods = _mod_call(cond, mod_w, mod_b).reshape(depth, MOD_ROWS, N_MOD, d)
    cs, cst = _rope_tables(seq)
    w_in, w_out = ffn_w_in.astype(BF16), ffn_w_out.astype(BF16)
    xc = ctx.reshape(1, bsz * n_ctx, d)
    x_row = lambda b: b
    c_row = lambda b: bsz
    c64 = HEAD_DIM ** -0.5 * LOG2E

    def per_batch(t):
        return t.reshape(bsz, n_ctx, t.shape[-1])

    for layer in range(depth):
        kind = layer % 4
        j = layer // 4
        last = layer == depth - 1
        ml = mods[layer]

        def ffn1(t, row, **proj):
            return _ffn(t, norm_g[layer, 0], ml, row, w_in, w_out, layer, 0, 0,
                        proj=dict(g=norm_g[layer, 1], base=3, **proj))

        if kind == 0:
            h = MLA_HEADS
            qk = MLA_Q_RANK + MLA_KV_RANK
            c_mla = (MLA_NOPE + ROPE_DIM) ** -0.5 * LOG2E
            wd = mla_w_down[j].astype(BF16)
            z = jnp.zeros((d, ROPE_DIM), BF16)
            w1x = jnp.concatenate([wd, z], axis=1)
            w1c = jnp.concatenate([wd[:, :qk], z, wd[:, qk:]], axis=1)
            n1 = w1x.shape[1] // LANES
            x, p1x = ffn1(x, x_row, w=w1x, kinds=[None] * (n1 - 1) + [True], cs=cs, out_dtype=F32)
            xc, p1c = ffn1(xc, c_row, w=w1c, kinds=[None] * n1, out_dtype=F32)
            wqt = mla_w_uq[j].astype(BF16).T
            wkv = mla_w_ukv[j].astype(BF16).reshape(MLA_KV_RANK, h, MLA_NOPE + MLA_V)
            wk = wkv[:, :, :MLA_NOPE].reshape(MLA_KV_RANK, -1)
            wvt = wkv[:, :, MLA_NOPE:].reshape(MLA_KV_RANK, -1).T
            kv_blk = MLA_Q_RANK // MLA_KV_RANK
            q_kinds, q_plain = [], []
            for hh in range(h):
                o4 = 4 * hh
                nope = [((o4, False, c_mla),), ((o4 + 1, False, c_mla),)]
                q_kinds += nope + [((o4 + 2, True, c_mla), (o4 + 3, False, c_mla))]
                q_plain += nope + [((o4 + 2, False, c_mla), (o4 + 3, False, c_mla))]
            v_kinds = _plan([(False, 1.0)] * (2 * h))
            k_none = [None] * h
            qtx = _proj(p1x, MLA_Q_RANK, 0, mla_q_norm[j], None, None, 0, wt=wqt, kinds_t=q_kinds, cst=cst)
            kvx, vtx = _proj(p1x, MLA_KV_RANK, kv_blk, mla_kv_norm[j], None, None, 0, w=wk, kinds=k_none,
                             wt=wvt, kinds_t=v_kinds)
            kvc, vtc = _proj(p1c, MLA_KV_RANK, kv_blk, mla_kv_norm[j], None, None, 0, w=wk, kinds=k_none,
                             wt=wvt, kinds_t=v_kinds)
            kvc, p1c_b = per_batch(kvc), per_batch(p1c)
            ox = _mla_attn(qtx, kvc, p1c_b, vtc, kvx, p1x, vtx)
            w_o = mla_w_o[j].astype(BF16)
            if not last:
                qtc = _proj(p1c, MLA_Q_RANK, 0, mla_q_norm[j], None, None, 0, wt=wqt, kinds_t=q_plain)
                oc = _mla_attn(qtc, kvc, p1c_b, vtc)
        elif kind == 1:
            wqkv = swa_w_qkv[j].astype(BF16)
            nq = SWA_KV_HEADS * SWA_GROUP * HEAD_DIM
            nk = SWA_KV_HEADS * HEAD_DIM
            perm = _swa_q_perm()
            wqt = wqkv[:, :nq][:, perm].T
            wk, wvt = wqkv[:, nq:nq + nk], wqkv[:, nq + nk:].T
            gq, gv = nq // HEAD_DIM, nk // HEAD_DIM
            wt = jnp.concatenate([wqt, wvt], axis=0)
            ktx = ([((g, True, c64), (gq + g, False, c64)) for g in range(gq)]
                   + [((2 * gq + g, False, 1.0),) for g in range(gv)])
            x, kx, ptx = ffn1(x, x_row, w=wk, kinds=[True] * (nk // LANES), cs=cs, wt=wt, kinds_t=ktx, cst=cst)
            xc, kc, ptc = ffn1(xc, c_row, w=wk, kinds=[None] * (nk // LANES), wt=wt,
                               kinds_t=_plan([(False, c64)] * gq + [(False, 1.0)] * gv))
            kc = per_batch(kc)
            ox = _swa_attn(swa_sink[j], kc, ptc, kx, ptx)
            w_o = swa_w_o[j].astype(BF16)[perm, :]
            if not last:
                oc = _swa_attn(swa_sink[j], kc, ptc)
        elif kind == 2:
            wqkv = na_w_qkv[j].astype(BF16)
            nh = NA_HEADS * HEAD_DIM
            wk = wqkv[:, nh:2 * nh]
            wt = jnp.concatenate([wqkv[:, :nh], wqkv[:, 2 * nh:]], axis=1).T
            kt = _plan([(False, c64)] * NA_HEADS + [(False, 1.0)] * NA_HEADS)
            k_none = [None] * (nh // LANES)
            x, kx, ptx = ffn1(x, x_row, w=wk, kinds=k_none, wt=wt, kinds_t=kt)
            xc, kc, ptc = ffn1(xc, c_row, w=wk, kinds=k_none, wt=wt, kinds_t=kt)
            kc = per_batch(kc)
            ox = _na_attn(kc, ptc, _na_bias_table(na_rpb[j], seq // GRID_W), kx, ptx)
            w_o = na_w_o[j].astype(BF16)
            if not last:
                oc = _na_attn(kc, ptc)
        else:
            h = DIFF_HEADS
            lam_init = 0.8 - 0.6 * math.exp(-0.3 * layer)
            wqkv = diff_w_qkv[j].astype(BF16)
            nq = 2 * h * HEAD_DIM
            wk = wqkv[:, nq:2 * nq]
            wqt = wqkv[:, :nq].T
            wvt = wqkv[:, 2 * nq:].T
            wt = jnp.concatenate([wqt, wvt], axis=0)
            ktx = ([((4 * (g // 2) + g % 2, True, c64), (4 * (g // 2) + 2 + g % 2, False, c64))
                    for g in range(2 * h)] + [((4 * h + g, False, 1.0),) for g in range(2 * h)])
            ktc = _plan([(False, c64)] * (2 * h) + [(False, 1.0)] * (2 * h))
            x, kx, ptx = ffn1(x, x_row, w=wk, kinds=[True] * h, cs=cs, wt=wt, kinds_t=ktx, cst=cst)
            xc, kc, ptc = ffn1(xc, c_row, w=wk, kinds=[None] * h, wt=wt, kinds_t=ktc)
            kc = per_batch(kc)
            ox = _diff_attn(kc, ptc, diff_lambda[j], diff_norm_g[j], lam_init, 0, h, kx, ptx)
            w_o = diff_w_o[j].astype(BF16)
            if not last:
                oc = _diff_attn(kc, ptc, diff_lambda[j], diff_norm_g[j], lam_init, 0, h)

        x = _ffn(x, norm_g[layer, 2], ml, x_row, w_in, w_out, layer, 1, 6,
                 final_g=final_norm_g if last else None, mixer=(ox, w_o, 5))
        if not last:
            oc = oc.reshape(1, bsz * n_ctx, oc.shape[-1])
            xc = _ffn(xc, norm_g[layer, 2], ml, c_row, w_in, w_out, layer, 1, 6, mixer=(oc, w_o, 5))
    return x
```
